```python
import math
import jax, jax.numpy as jnp
from jax import lax
import numpy as np


D_MODEL = 1024
BATCH = 16
SEQ = 4096
DEPTH = 4
DEC_BATCH = 2
DEC_SEQ = 16384
PAST_LEN = 128

D_SSM = 512
SSM_GROUP = 16
N_SSM_GROUPS = D_SSM // SSM_GROUP
SSM_STATE = 64
N_DIR = 2
DT_MIN = 1e-3
DT_MAX = 1e-1
N_FOURIER_GROUPS = 4
FOURIER_GROUP = 64
D_FOURIER = N_FOURIER_GROUPS * FOURIER_GROUP
POOL_WINDOWS = (2, 4, 8, 16)
N_POOL_GROUPS = len(POOL_WINDOWS)
POOL_GROUP = 64
D_POOL = N_POOL_GROUPS * POOL_GROUP
N_BRANCHES = 3
D_MIX = D_SSM + D_FOURIER + D_POOL
D_IN_PROJ = D_MIX + N_BRANCHES * D_MODEL
D_FF = int(math.ceil(8 * D_MODEL / 3 / 256)) * 256
EPS = 1e-6

kernel_name = 'hybrid_s5_fourier_pool_encoder'

F32 = jnp.float32


def rmsnorm(x, g):
    xf = x.astype(F32)
    y = xf * lax.rsqrt(jnp.mean(xf * xf, axis=-1, keepdims=True) + EPS)
    return (y * g.astype(F32)).astype(x.dtype)


def _linear_scan(bu, lam_bar, reverse):
    a = jnp.broadcast_to(lam_bar, bu.shape)

    def combine(e1, e2):
        a1, b1 = e1
        a2, b2 = e2
        return a2 * a1, a2 * b1 + b2

    _, h = lax.associative_scan(combine, (a, bu), reverse=reverse, axis=1)
    return h


def s5_mixer(u, lam_re, lam_im, log_step, b_re, b_im, c_re, c_im, d_skip, w_glu, b_glu):
    bsz, l, _ = u.shape
    uf = u.astype(F32)
    ug = uf.reshape(bsz, l, N_SSM_GROUPS, SSM_GROUP).astype(jnp.complex64)
    b_mat = lax.complex(b_re.astype(F32), b_im.astype(F32))
    y = d_skip.astype(F32) * uf
    for direction, reverse in ((0, False), (1, True)):
        lam = lax.complex(jnp.minimum(lam_re[direction].astype(F32), -1e-4), lam_im[direction].astype(F32))
        step = jnp.exp(log_step[direction].astype(F32))[:, None]
        lam_bar = jnp.exp(lam * step)
        b_bar = ((lam_bar - 1.0) / lam)[:, :, None] * b_mat
        bu = jnp.einsum('blgc,gpc->blgp', ug, b_bar)
        h = _linear_scan(bu, lam_bar, reverse)
        c_mat = lax.complex(c_re[direction].astype(F32), c_im[direction].astype(F32))
        y = y + jnp.einsum('blgp,gcp->blgc', h, c_mat).real.reshape(bsz, l, D_SSM)
    g = jax.nn.gelu(y)
    out = g * jax.nn.sigmoid(g @ w_glu.astype(F32) + b_glu.astype(F32))
    return out.astype(u.dtype)


def fourier_mixer(u):
    bsz, l, _ = u.shape
    ug = u.astype(F32).reshape(bsz, l, N_FOURIER_GROUPS, FOURIER_GROUP)
    f = jnp.fft.fft2(ug, axes=(1, 3), norm='ortho').real
    return f.reshape(bsz, l, D_FOURIER).astype(u.dtype)


def pool_mixer(u, w_pool, pool_scale):
    bsz, l, _ = u.shape
    uf = u.astype(F32).reshape(bsz, l, N_POOL_GROUPS, POOL_GROUP)
    csum = jnp.concatenate([jnp.zeros((bsz, 1, N_POOL_GROUPS, POOL_GROUP), F32), jnp.cumsum(uf, axis=1)], axis=1)
    pos = np.arange(l)
    outs = []
    for gi, w in enumerate(POOL_WINDOWS):
        lo = np.clip(pos - w // 2, 0, l)
        hi = np.clip(pos + w // 2, 0, l)
        inv_cnt = (1.0 / (hi - lo)).astype(np.float32)[None, :, None]
        cg = csum[:, :, gi]
        outs.append((cg[:, hi] - cg[:, lo]) * inv_cnt - uf[:, :, gi])
    p = jnp.stack(outs, axis=2)
    mixed = jnp.einsum('blgc,gcd->blgd', p, w_pool.astype(F32)).reshape(bsz, l, D_POOL)
    return (mixed * pool_scale.astype(F32)).astype(u.dtype)


def _trunk(x, norm_mix, w_in, ssm_lam_re, ssm_lam_im, ssm_log_step, ssm_b_re, ssm_b_im, ssm_c_re, ssm_c_im, ssm_d, w_glu, b_glu, w_up_ssm, w_up_fourier, w_pool, pool_scale, w_up_pool, w_out, norm_ffn, w_ff_gate, w_ff_up, w_ff_down, norm_final):
    h = x
    for i in range(DEPTH):
        z = rmsnorm(h, norm_mix[i]) @ w_in[i]
        u_ssm = z[..., :D_SSM]
        u_four = z[..., D_SSM:D_SSM + D_FOURIER]
        u_pool = z[..., D_SSM + D_FOURIER:D_MIX]
        gates = jax.nn.sigmoid(z[..., D_MIX:].astype(F32)).astype(z.dtype)
        y_ssm = s5_mixer(u_ssm, ssm_lam_re[i], ssm_lam_im[i], ssm_log_step[i], ssm_b_re[i], ssm_b_im[i], ssm_c_re[i], ssm_c_im[i], ssm_d[i], w_glu[i], b_glu[i]) @ w_up_ssm[i]
        y_four = fourier_mixer(u_four) @ w_up_fourier[i]
        y_pool = pool_mixer(u_pool, w_pool[i], pool_scale[i]) @ w_up_pool[i]
        merged = (gates[..., :D_MODEL] * y_ssm + gates[..., D_MODEL:2 * D_MODEL] * y_four + gates[..., 2 * D_MODEL:] * y_pool)
        h = h + merged @ w_out[i]
        hn = rmsnorm(h, norm_ffn[i])
        h = h + (jax.nn.silu(hn @ w_ff_gate[i]) * (hn @ w_ff_up[i])) @ w_ff_down[i]
    return rmsnorm(h, norm_final)


def setup_inputs(seed: int = 0) -> dict:
    key = jax.random.key(seed)
    ks = jax.random.split(key, 32)
    nrm = jax.random.normal
    G, P = N_SSM_GROUPS, SSM_STATE
    n_idx = jnp.arange(P, dtype=F32)
    return {
        'x_prompt': nrm(ks[0], (BATCH, SEQ, D_MODEL), F32),
        'x_sample': nrm(ks[1], (DEC_BATCH, DEC_SEQ, D_MODEL), F32),
        'norm_mix': 1.0 + 0.02 * nrm(ks[2], (DEPTH, D_MODEL), F32),
        'w_in': nrm(ks[3], (DEPTH, D_MODEL, D_IN_PROJ), F32) * D_MODEL ** -0.5,
        'ssm_lam_re': -0.5 + 0.01 * nrm(ks[4], (DEPTH, N_DIR, G, P), F32),
        'ssm_lam_im': math.pi * n_idx + 0.01 * nrm(ks[5], (DEPTH, N_DIR, G, P), F32),
        'ssm_log_step': jax.random.uniform(ks[6], (DEPTH, N_DIR, G), F32, math.log(DT_MIN), math.log(DT_MAX)),
        'ssm_b_re': nrm(ks[7], (DEPTH, G, P, SSM_GROUP), F32) * (2 * SSM_GROUP) ** -0.5,
        'ssm_b_im': nrm(ks[8], (DEPTH, G, P, SSM_GROUP), F32) * (2 * SSM_GROUP) ** -0.5,
        'ssm_c_re': nrm(ks[9], (DEPTH, N_DIR, G, SSM_GROUP, P), F32) * P ** -0.5,
        'ssm_c_im': nrm(ks[10], (DEPTH, N_DIR, G, SSM_GROUP, P), F32) * P ** -0.5,
        'ssm_d': nrm(ks[11], (DEPTH, D_SSM), F32),
        'w_glu': nrm(ks[12], (DEPTH, D_SSM, D_SSM), F32) * D_SSM ** -0.5,
        'b_glu': 0.01 * nrm(ks[13], (DEPTH, D_SSM), F32),
        'w_up_ssm': nrm(ks[14], (DEPTH, D_SSM, D_MODEL), F32) * D_SSM ** -0.5,
        'w_up_fourier': nrm(ks[15], (DEPTH, D_FOURIER, D_MODEL), F32) * D_FOURIER ** -0.5,
        'w_pool': nrm(ks[16], (DEPTH, N_POOL_GROUPS, POOL_GROUP, POOL_GROUP), F32) * POOL_GROUP ** -0.5,
        'pool_scale': 1.0 + 0.02 * nrm(ks[17], (DEPTH, D_POOL), F32),
        'w_up_pool': nrm(ks[18], (DEPTH, D_POOL, D_MODEL), F32) * D_POOL ** -0.5,
        'w_out': nrm(ks[19], (DEPTH, D_MODEL, D_MODEL), F32) * D_MODEL ** -0.5,
        'norm_ffn': 1.0 + 0.02 * nrm(ks[20], (DEPTH, D_MODEL), F32),
        'w_ff_gate': nrm(ks[21], (DEPTH, D_MODEL, D_FF), F32) * D_MODEL ** -0.5,
        'w_ff_up': nrm(ks[22], (DEPTH, D_MODEL, D_FF), F32) * D_MODEL ** -0.5,
        'w_ff_down': nrm(ks[23], (DEPTH, D_FF, D_MODEL), F32) * D_FF ** -0.5,
        'norm_final': 1.0 + 0.02 * nrm(ks[24], (D_MODEL,), F32),
    }


def reference(x_prompt, x_sample, norm_mix, w_in, ssm_lam_re, ssm_lam_im, ssm_log_step, ssm_b_re, ssm_b_im, ssm_c_re, ssm_c_im, ssm_d, w_glu, b_glu, w_up_ssm, w_up_fourier, w_pool, pool_scale, w_up_pool, w_out, norm_ffn, w_ff_gate, w_ff_up, w_ff_down, norm_final):
    y_prompt = _trunk(x_prompt, norm_mix, w_in, ssm_lam_re, ssm_lam_im, ssm_log_step, ssm_b_re, ssm_b_im, ssm_c_re, ssm_c_im, ssm_d, w_glu, b_glu, w_up_ssm, w_up_fourier, w_pool, pool_scale, w_up_pool, w_out, norm_ffn, w_ff_gate, w_ff_up, w_ff_down, norm_final)
    y_sample = _trunk(x_sample, norm_mix, w_in, ssm_lam_re, ssm_lam_im, ssm_log_step, ssm_b_re, ssm_b_im, ssm_c_re, ssm_c_im, ssm_d, w_glu, b_glu, w_up_ssm, w_up_fourier, w_pool, pool_scale, w_up_pool, w_out, norm_ffn, w_ff_gate, w_ff_up, w_ff_down, norm_final)
    return (y_prompt, y_sample)
```

```python
import functools
import math

import jax
import jax.numpy as jnp
import numpy as np
from jax import lax
from jax.experimental import pallas as pl
from jax.experimental.pallas import tpu as pltpu

F32 = jnp.float32
BF16 = jnp.bfloat16

D_MODEL = 1024
DEPTH = 4
D_SSM = 512
SSM_GROUP = 16
N_SSM_GROUPS = D_SSM // SSM_GROUP
SSM_STATE = 64
D_FOURIER = 256
FOURIER_GROUP = 64
D_POOL = 256
POOL_GROUP = 64
D_MIX = D_SSM + D_FOURIER + D_POOL
D_FF = 2816
EPS = 1e-6

CHUNK = 16
SCAN_ROWS = 16
PREP_GROUPS = 8
FF_CHUNK = 256
TOKEN_TILE = 512
VMEM_LIMIT = 58 * 1024 * 1024


def _dot(a, b):
    return jnp.dot(a, b, preferred_element_type=F32)


def _rms(x, g):
    return x * lax.rsqrt(jnp.mean(x * x, axis=-1, keepdims=True) + EPS) * g


def _const_spec(shape):
    nd = len(shape)
    return pl.BlockSpec(shape, lambda *_: (0,) * nd, pipeline_mode=pl.Buffered(1))


def _params(n_axes):
    return pltpu.CompilerParams(
        dimension_semantics=("arbitrary",) * n_axes, vmem_limit_bytes=VMEM_LIMIT
    )


def _emit_in_proj(xb, w_ref, us_ref, uf_ref, up_ref, gt_ref):
    us_ref[...] = _dot(xb, w_ref[:, 0:D_SSM]).astype(BF16)
    uf_ref[...] = _dot(xb, w_ref[:, D_SSM : D_SSM + D_FOURIER]).astype(BF16)
    up_ref[...] = _dot(xb, w_ref[:, D_SSM + D_FOURIER : D_MIX]).astype(BF16)
    for j in range(3):
        lo = D_MIX + j * D_MODEL
        z = _dot(xb, w_ref[:, lo : lo + D_MODEL])
        gt_ref[:, j * D_MODEL : (j + 1) * D_MODEL] = jax.nn.sigmoid(z).astype(BF16)


def _in_proj_body(h_ref, g_ref, w_ref, us_ref, uf_ref, up_ref, gt_ref):
    xb = _rms(h_ref[...], g_ref[...]).astype(BF16)
    _emit_in_proj(xb, w_ref, us_ref, uf_ref, up_ref, gt_ref)


def _in_proj_out(n, tm):
    shapes = (
        jax.ShapeDtypeStruct((n, D_SSM), BF16),
        jax.ShapeDtypeStruct((n, D_FOURIER), BF16),
        jax.ShapeDtypeStruct((n, D_POOL), BF16),
        jax.ShapeDtypeStruct((n, 3 * D_MODEL), BF16),
    )
    specs = (
        pl.BlockSpec((tm, D_SSM), lambda i: (i, 0)),
        pl.BlockSpec((tm, D_FOURIER), lambda i: (i, 0)),
        pl.BlockSpec((tm, D_POOL), lambda i: (i, 0)),
        pl.BlockSpec((tm, 3 * D_MODEL), lambda i: (i, 0)),
    )
    return shapes, specs


def _in_proj(h, g, w_in, tm):
    n = h.shape[0]
    shapes, specs = _in_proj_out(n, tm)
    return pl.pallas_call(
        _in_proj_body,
        grid=(n // tm,),
        in_specs=[
            pl.BlockSpec((tm, D_MODEL), lambda i: (i, 0)),
            _const_spec((1, D_MODEL)),
            _const_spec((D_MODEL, D_MIX + 3 * D_MODEL)),
        ],
        out_specs=specs,
        out_shape=shapes,
        compiler_params=_params(1),
        name="in_proj",
    )(h, g, w_in)


def _s5_prep_body(lre_ref, lim_ref, ls_ref, btr_ref, bti_ref, cr_ref, ci_ref, d_ref,
                  w_ref, qt_ref, km_ref, a_ref):
    is_fwd = (pl.program_id(1) == 0).astype(F32)
    lre = jnp.minimum(lre_ref[0, 0], -1e-4)
    lim = lim_ref[0, 0]
    step = jnp.exp(ls_ref[0, 0])
    mag = jnp.exp(lre * step)
    lbr = mag * jnp.cos(lim * step)
    lbi = mag * jnp.sin(lim * step)
    den = lre * lre + lim * lim
    nr = lbr - 1.0
    kr = (nr * lre + lbi * lim) / den
    ki = (lbi * lre - nr * lim) / den
    btr = btr_ref[0]
    bti = bti_ref[0]
    bbr = kr * btr - ki * bti
    bbi = kr * bti + ki * btr
    cr = cr_ref[0, 0]
    ci = ci_ref[0, 0]
    gb = cr.shape[0]
    eye = (
        lax.broadcasted_iota(jnp.int32, (gb, SSM_GROUP, SSM_GROUP), 1)
        == lax.broadcasted_iota(jnp.int32, (gb, SSM_GROUP, SSM_GROUP), 2)
    ).astype(F32)
    d_diag = eye * d_ref[0] * is_fwd
    pr = jnp.ones_like(lbr)
    pi = jnp.zeros_like(lbr)
    for j in range(CHUNK):
        rows = slice(j * SSM_GROUP, (j + 1) * SSM_GROUP)
        wr = pr * bbr - pi * bbi
        wi = pr * bbi + pi * bbr
        w_ref[0, 0, 0, :, rows, :] = wr
        w_ref[0, 0, 1, :, rows, :] = wi
        km = jnp.einsum("gap,gcp->gac", wr, cr, precision=lax.Precision.HIGHEST,
                        preferred_element_type=F32)
        km = km - jnp.einsum("gap,gcp->gac", wi, ci, precision=lax.Precision.HIGHEST,
                             preferred_element_type=F32)
        if j == 0:
            km = km + d_diag
        km_ref[0, 0, :, rows, :] = km
        pr, pi = pr * lbr - pi * lbi, pr * lbi + pi * lbr
        qt_ref[0, 0, 0, :, rows, :] = cr * pr - ci * pi
        qt_ref[0, 0, 1, :, rows, :] = -(cr * pi + ci * pr)
    a_ref[0, 0, 0] = pr
    a_ref[0, 0, 1] = pi


def _s5_prep(lam_re, lam_im, log_step, b_re, b_im, c_re, c_im, d_skip):
    g, p, gb = N_SSM_GROUPS, SSM_STATE, PREP_GROUPS
    lre = lam_re.reshape(DEPTH, 2, g, 1, p)
    lim = lam_im.reshape(DEPTH, 2, g, 1, p)
    ls = jnp.broadcast_to(log_step.reshape(DEPTH, 2, g, 1, 1), (DEPTH, 2, g, 1, p))
    btr = jnp.swapaxes(b_re, 2, 3)
    bti = jnp.swapaxes(b_im, 2, 3)
    d4 = d_skip.reshape(DEPTH, g, 1, SSM_GROUP)
    pole_spec = pl.BlockSpec((1, 1, gb, 1, p), lambda i, d, j: (i, d, j, 0, 0))
    b_spec = pl.BlockSpec((1, gb, SSM_GROUP, p), lambda i, d, j: (i, j, 0, 0))
    c_spec = pl.BlockSpec((1, 1, gb, SSM_GROUP, p), lambda i, d, j: (i, d, j, 0, 0))
    rows = CHUNK * SSM_GROUP
    return pl.pallas_call(
        _s5_prep_body,
        grid=(DEPTH, 2, g // gb),
        in_specs=[pole_spec, pole_spec, pole_spec, b_spec, b_spec, c_spec, c_spec,
                  pl.BlockSpec((1, gb, 1, SSM_GROUP), lambda i, d, j: (i, j, 0, 0))],
        out_specs=(
            pl.BlockSpec((1, 1, 2, gb, rows, p), lambda i, d, j: (i, d, 0, j, 0, 0)),
            pl.BlockSpec((1, 1, 2, gb, rows, p), lambda i, d, j: (i, d, 0, j, 0, 0)),
            pl.BlockSpec((1, 1, gb, rows, SSM_GROUP), lambda i, d, j: (i, d, j, 0, 0)),
            pl.BlockSpec((1, 1, 2, gb, 1, p), lambda i, d, j: (i, d, 0, j, 0, 0)),
        ),
        out_shape=(
            jax.ShapeDtypeStruct((DEPTH, 2, 2, g, rows, p), F32),
            jax.ShapeDtypeStruct((DEPTH, 2, 2, g, rows, p), F32),
            jax.ShapeDtypeStruct((DEPTH, 2, g, rows, SSM_GROUP), F32),
            jax.ShapeDtypeStruct((DEPTH, 2, 2, g, 1, p), F32),
        ),
        compiler_params=_params(3),
        name="s5_prep",
    )(lre, lim, ls, btr, bti, c_re, c_im, d4)


def _s5_assemble(w, qt, km, a16):
    g, p, t, c = N_SSM_GROUPS, SSM_STATE, CHUNK, SSM_GROUP
    w = w.reshape(DEPTH, 2, 2, g, t, c, p)
    qt = qt.reshape(DEPTH, 2, 2, g, t, c, p)
    pf = w[:, 0, :, :, ::-1]
    pb = w[:, 1]
    pcomp = jnp.stack([pf[:, 0], pf[:, 1], pb[:, 0], pb[:, 1]], axis=1)
    pcomp = pcomp.reshape(DEPTH, 4, g // 2, 2, t * c, p)
    pair_eye = jnp.eye(2, dtype=bool)[None, None, None, :, None, :, None]
    pmat = jnp.where(pair_eye, pcomp[:, :, :, :, :, None, :], 0.0)
    pmat = jnp.transpose(pmat, (0, 2, 3, 4, 1, 5, 6)).reshape(DEPTH, g // 2, 2 * t * c, 4 * 2 * p)
    qf = qt[:, 0]
    qb = qt[:, 1, :, :, ::-1]
    qcomp = jnp.stack([qf[:, 0], qf[:, 1], qb[:, 0], qb[:, 1]], axis=1)
    qcomp = qcomp.reshape(DEPTH, 4, g // 2, 2, t * c, p)
    qmat = jnp.where(pair_eye, qcomp[:, :, :, :, :, None, :], 0.0)
    qmat = jnp.transpose(qmat, (0, 2, 3, 1, 5, 6, 4)).reshape(DEPTH, g, 4 * 2 * p, t * c)
    km = km.reshape(DEPTH, 2, g, t, c, c)
    s_idx = np.arange(t)[:, None]
    t_idx = np.arange(t)[None, :]
    mf = jnp.where((t_idx >= s_idx)[None, None, :, :, None, None],
                   km[:, 0][:, :, np.clip(t_idx - s_idx, 0, t - 1)], 0.0)
    mb = jnp.where((t_idx <= s_idx)[None, None, :, :, None, None],
                   km[:, 1][:, :, np.clip(s_idx - t_idx, 0, t - 1)], 0.0)
    mmat = jnp.stack([mf, mb], axis=2)
    mmat = jnp.transpose(mmat, (0, 1, 2, 3, 5, 4, 6)).reshape(DEPTH, g, 2, t * c, t * c)
    acoef = a16.reshape(DEPTH, 4, g // 2, 2 * p)
    acoef = jnp.transpose(acoef, (0, 2, 1, 3))
    acoef = jnp.concatenate([acoef, jnp.zeros_like(acoef)], axis=2)
    return pmat.astype(BF16), mmat.astype(BF16), qmat.astype(BF16), acoef


def _cmul(ar, ai, br, bi):
    return ar * br - ai * bi, ar * bi + ai * br


def _cpow(ar, ai, n):
    rr, ri = jnp.ones_like(ar), jnp.zeros_like(ar)
    while n:
        if n & 1:
            rr, ri = _cmul(rr, ri, ar, ai)
        ar, ai = _cmul(ar, ai, ar, ai)
        n >>= 1
    return rr, ri


def _s5_body(x_ref, p_ref, m_ref, q_ref, a_ref, y_ref, sh_ref, *, n_chunks, n_seg, row_block):
    rows = n_chunks * SCAN_ROWS
    lanes = 2 * SSM_STATE
    for r0 in range(0, rows, row_block):
        rs = slice(r0, r0 + row_block)
        sh_ref[rs, :] = _dot(x_ref[0, rs, :], p_ref[0, 0:256, :]) + _dot(x_ref[1, rs, :], p_ref[0, 256:512, :])

    def coef(i):
        return jnp.broadcast_to(a_ref[0, i : i + 1, :], (SCAN_ROWS, lanes))

    afr, afi, abr, abi = coef(0), coef(1), coef(2), coef(3)
    zero = jnp.zeros((SCAN_ROWS, lanes), F32)

    def tile(k, j):
        return (pl.ds(pl.multiple_of(k * SCAN_ROWS, SCAN_ROWS), SCAN_ROWS), slice(j * lanes, (j + 1) * lanes))

    def scan(k, carry):
        fr, fi, br, bi = carry
        kb = n_chunks - 1 - k
        sfr, sfi = sh_ref[tile(k, 0)], sh_ref[tile(k, 1)]
        sbr, sbi = sh_ref[tile(kb, 2)], sh_ref[tile(kb, 3)]
        sh_ref[tile(k, 0)] = fr
        sh_ref[tile(k, 1)] = fi
        sh_ref[tile(kb, 2)] = br
        sh_ref[tile(kb, 3)] = bi
        nfr, nfi = _cmul(afr, afi, fr, fi)
        nbr, nbi = _cmul(abr, abi, br, bi)
        return nfr + sfr, nfi + sfi, nbr + sbr, nbi + sbi

    fr, fi, br, bi = lax.fori_loop(0, n_chunks, scan, (zero, zero, zero, zero))

    if n_seg > 1:
        seg = lax.broadcasted_iota(jnp.int32, (SCAN_ROWS, lanes), 0) % n_seg
        akfr, akfi = _cpow(afr, afi, n_chunks)
        akbr, akbi = _cpow(abr, abi, n_chunks)
        cfr, cfi, cbr, cbi = zero, zero, zero, zero
        for _ in range(n_seg - 1):
            tr, ti = _cmul(akfr, akfi, cfr, cfi)
            cfr = jnp.where(seg == 0, 0.0, pltpu.roll(tr + fr, 1, 0))
            cfi = jnp.where(seg == 0, 0.0, pltpu.roll(ti + fi, 1, 0))
            tr, ti = _cmul(akbr, akbi, cbr, cbi)
            cbr = jnp.where(seg == n_seg - 1, 0.0, pltpu.roll(tr + br, SCAN_ROWS - 1, 0))
            cbi = jnp.where(seg == n_seg - 1, 0.0, pltpu.roll(ti + bi, SCAN_ROWS - 1, 0))

        def fix(k, carry):
            pfr, pfi, pbr, pbi = carry
            kb = n_chunks - 1 - k
            dr, di = _cmul(pfr, pfi, cfr, cfi)
            sh_ref[tile(k, 0)] += dr
            sh_ref[tile(k, 1)] += di
            dr, di = _cmul(pbr, pbi, cbr, cbi)
            sh_ref[tile(kb, 2)] += dr
            sh_ref[tile(kb, 3)] += di
            return _cmul(afr, afi, pfr, pfi) + _cmul(abr, abi, pbr, pbi)

        one = jnp.ones((SCAN_ROWS, lanes), F32)
        lax.fori_loop(0, n_chunks, fix, (one, zero, one, zero))

    for r0 in range(0, rows, row_block):
        rs = slice(r0, r0 + row_block)
        hb = sh_ref[rs, :].astype(BF16)
        for gi in range(2):
            x = x_ref[gi, rs, :]
            y_ref[gi, rs, :] = _dot(x, m_ref[gi, 0]) + _dot(x, m_ref[gi, 1]) + _dot(hb, q_ref[gi])


def _s5_mix(x, pmat, mmat, qmat, acoef, n_chunks, n_seg):
    g = N_SSM_GROUPS
    rows = n_chunks * SCAN_ROWS
    width = CHUNK * SSM_GROUP
    body = functools.partial(_s5_body, n_chunks=n_chunks, n_seg=n_seg, row_block=min(rows, 1024))
    return pl.pallas_call(
        body,
        grid=(g // 2,),
        in_specs=[
            pl.BlockSpec((2, rows, width), lambda i: (i, 0, 0)),
            pl.BlockSpec((1, 2 * width, 8 * SSM_STATE), lambda i: (i, 0, 0)),
            pl.BlockSpec((2, 2, width, width), lambda i: (i, 0, 0, 0)),
            pl.BlockSpec((2, 8 * SSM_STATE, width), lambda i: (i, 0, 0)),
            pl.BlockSpec((1, 8, 2 * SSM_STATE), lambda i: (i, 0, 0)),
        ],
        out_specs=pl.BlockSpec((2, rows, width), lambda i: (i, 0, 0)),
        out_shape=jax.ShapeDtypeStruct((g, rows, width), F32),
        scratch_shapes=[pltpu.VMEM((rows, 8 * SSM_STATE), F32)],
        compiler_params=_params(1),
        name="s5_mix",
    )(x, pmat, mmat, qmat, acoef)


def _s5_layout(bsz, seq):
    n_seg = SCAN_ROWS // bsz
    n_chunks = seq // (CHUNK * n_seg)
    assert bsz * n_seg == SCAN_ROWS and n_chunks * n_seg * CHUNK == seq
    return n_seg, n_chunks


def _s5_to_chunks(u, bsz, seq):
    n_seg, n_chunks = _s5_layout(bsz, seq)
    x = u.reshape(bsz, n_seg, n_chunks, CHUNK, N_SSM_GROUPS, SSM_GROUP)
    x = jnp.transpose(x, (4, 2, 0, 1, 3, 5))
    return x.reshape(N_SSM_GROUPS, n_chunks * SCAN_ROWS, CHUNK * SSM_GROUP)


def _s5_from_chunks(y, bsz, seq):
    n_seg, n_chunks = _s5_layout(bsz, seq)
    y = y.reshape(N_SSM_GROUPS, n_chunks, bsz, n_seg, CHUNK, SSM_GROUP)
    y = jnp.transpose(y, (2, 3, 1, 4, 0, 5))
    return y.reshape(bsz * seq, D_SSM)


def _dft_split(seq):
    l1 = 1 << (int(math.log2(seq)) // 2)
    return l1, seq // l1


def _dft_tables(seq):
    l1, l2 = _dft_split(seq)
    k1 = (jnp.arange(l1)[:, None] * jnp.arange(l1)[None, :]) % l1
    th1 = k1.astype(F32) * (2.0 * math.pi / l1)
    f1cat = jnp.concatenate([jnp.cos(th1), -jnp.sin(th1)], axis=0)
    f = jnp.arange(l1)[:, None, None] + l1 * jnp.arange(l2)[None, :, None]
    k2 = (f * jnp.arange(l2)[None, None, :]) % seq
    th2 = k2.astype(F32) * (2.0 * math.pi / seq)
    c2, s2 = jnp.cos(th2), jnp.sin(th2)
    e2 = jnp.concatenate(
        [jnp.concatenate([c2, s2], axis=2), jnp.concatenate([-s2, c2], axis=2)], axis=1
    )
    return f1cat.astype(BF16), e2.astype(BF16)


def _channel_dft_table():
    n = FOURIER_GROUP
    th = ((jnp.arange(n)[:, None] * jnp.arange(n)[None, :]) % n).astype(F32) * (2.0 * math.pi / n)
    eye = jnp.eye(D_FOURIER // n, dtype=F32)
    return jnp.concatenate([jnp.kron(eye, jnp.cos(th)), jnp.kron(eye, jnp.sin(th))], axis=0).astype(BF16)


def _dft1_body(f_ref, v_ref, a_ref):
    a_ref[0] = _dot(f_ref[...], v_ref[0]).astype(BF16)


def _dft2_body(e_ref, a_ref, o_ref, *, n_f1):
    for i in range(n_f1):
        o_ref[0, :, i * D_FOURIER : (i + 1) * D_FOURIER] = _dot(e_ref[i], a_ref[0, i]).astype(BF16)


def _fourier_mix(u, f1cat, e2, bsz, seq):
    l1, l2 = _dft_split(seq)
    cols = l2 * D_FOURIER
    tn = min(cols, 4096)
    v = u.reshape(bsz, l1, cols)
    a = pl.pallas_call(
        _dft1_body,
        grid=(bsz, cols // tn),
        in_specs=[_const_spec((2 * l1, l1)), pl.BlockSpec((1, l1, tn), lambda b, j: (b, 0, j))],
        out_specs=pl.BlockSpec((1, 2 * l1, tn), lambda b, j: (b, 0, j)),
        out_shape=jax.ShapeDtypeStruct((bsz, 2 * l1, cols), BF16),
        compiler_params=_params(2),
        name="dft_stage1",
    )(f1cat, v)
    a = a.reshape(bsz, 2, l1, l2, D_FOURIER)
    a = jnp.transpose(a, (0, 2, 1, 3, 4)).reshape(bsz, l1, 2 * l2, D_FOURIER)
    n_f1 = 8
    w = pl.pallas_call(
        functools.partial(_dft2_body, n_f1=n_f1),
        grid=(l1 // n_f1, bsz),
        in_specs=[
            pl.BlockSpec((n_f1, 2 * l2, 2 * l2), lambda j, b: (j, 0, 0)),
            pl.BlockSpec((1, n_f1, 2 * l2, D_FOURIER), lambda j, b: (b, j, 0, 0)),
        ],
        out_specs=pl.BlockSpec((1, 2 * l2, n_f1 * D_FOURIER), lambda j, b: (b, 0, j)),
        out_shape=jax.ShapeDtypeStruct((bsz, 2 * l2, l1 * D_FOURIER), BF16),
        compiler_params=_params(2),
        name="dft_stage2",
    )(e2, a)
    return w.reshape(bsz, 2, seq, D_FOURIER)


def _pool_windows(up, prev, nxt, t0, seq):
    tm = up.shape[0]
    ext_rows = tm + 16
    ext = jnp.concatenate([prev, up, nxt], axis=0)
    s2 = ext + pltpu.roll(ext, 1, 0)
    s4 = s2 + pltpu.roll(s2, 2, 0)
    s8 = s4 + pltpu.roll(s4, 4, 0)
    s16 = s8 + pltpu.roll(s8, 8, 0)
    p2 = s2[8 : tm + 8]
    p4 = pltpu.roll(s4, ext_rows - 1, 0)[8 : tm + 8]
    p8 = pltpu.roll(s8, ext_rows - 3, 0)[8 : tm + 8]
    p16 = pltpu.roll(s16, ext_rows - 7, 0)[8 : tm + 8]
    grp = lax.broadcasted_iota(jnp.int32, (tm, D_POOL), 1) // POOL_GROUP
    t = t0 + lax.broadcasted_iota(jnp.int32, (tm, D_POOL), 0)
    half = jnp.left_shift(1, grp)
    cnt = jnp.minimum(t + half, seq) - jnp.maximum(t - half, 0)
    pooled = jnp.where(grp == 0, p2, jnp.where(grp == 1, p4, jnp.where(grp == 2, p8, p16)))
    return pooled / cnt.astype(F32) - up


def _merge_body(y_ref, w_ref, up_ref, upp_ref, upn_ref, gt_ref, h_ref,
                wglu_ref, bglu_ref, wups_ref, cs_ref, wupf_ref, wpool_ref, psc_ref, wupp_ref, wout_ref,
                o_ref, *, tm, seq, four_scale):
    t0 = (pl.program_id(0) % (seq // tm)) * tm
    g = jax.nn.gelu(y_ref[...])
    s = g * jax.nn.sigmoid(_dot(g.astype(BF16), wglu_ref[...]) + bglu_ref[...])
    merged = gt_ref[:, 0:D_MODEL].astype(F32) * _dot(s.astype(BF16), wups_ref[...])
    four = _dot(w_ref[0, 0], cs_ref[0:D_FOURIER, :]) + _dot(w_ref[0, 1], cs_ref[D_FOURIER:, :])
    four = four * four_scale
    merged += gt_ref[:, D_MODEL : 2 * D_MODEL].astype(F32) * _dot(four.astype(BF16), wupf_ref[...])
    up = up_ref[...].astype(F32)
    prev = jnp.where(t0 == 0, 0.0, upp_ref[8:16, :].astype(F32))
    nxt = jnp.where(t0 + tm == seq, 0.0, upn_ref[0:8, :].astype(F32))
    pw = _pool_windows(up, prev, nxt, t0, seq)
    pm = _dot(pw.astype(BF16), wpool_ref[...]) * psc_ref[...]
    merged += gt_ref[:, 2 * D_MODEL :].astype(F32) * _dot(pm.astype(BF16), wupp_ref[...])
    o_ref[...] = h_ref[...] + _dot(merged.astype(BF16), wout_ref[...])


def _merge(y, w, up, gates, h, lw, bsz, seq, tm):
    n = bsz * seq
    per_seq = seq // tm
    halo = tm // 16
    body = functools.partial(_merge_body, tm=tm, seq=seq, four_scale=1.0 / math.sqrt(seq * FOURIER_GROUP))
    return pl.pallas_call(
        body,
        grid=(n // tm,),
        in_specs=[
            pl.BlockSpec((tm, D_SSM), lambda i: (i, 0)),
            pl.BlockSpec((1, 2, tm, D_FOURIER), lambda i: (i // per_seq, 0, i % per_seq, 0)),
            pl.BlockSpec((tm, D_POOL), lambda i: (i, 0)),
            pl.BlockSpec((16, D_POOL), lambda i: (jnp.maximum(i * halo - 1, 0), 0)),
            pl.BlockSpec((16, D_POOL), lambda i: (jnp.minimum((i + 1) * halo, n // 16 - 1), 0)),
            pl.BlockSpec((tm, 3 * D_MODEL), lambda i: (i, 0)),
            pl.BlockSpec((tm, D_MODEL), lambda i: (i, 0)),
            _const_spec((D_SSM, D_SSM)),
            _const_spec((1, D_SSM)),
            _const_spec((D_SSM, D_MODEL)),
            _const_spec((2 * D_FOURIER, D_FOURIER)),
            _const_spec((D_FOURIER, D_MODEL)),
            _const_spec((D_POOL, D_POOL)),
            _const_spec((1, D_POOL)),
            _const_spec((D_POOL, D_MODEL)),
            _const_spec((D_MODEL, D_MODEL)),
        ],
        out_specs=pl.BlockSpec((tm, D_MODEL), lambda i: (i, 0)),
        out_shape=jax.ShapeDtypeStruct((n, D_MODEL), F32),
        compiler_params=_params(1),
        name="merge",
    )(y, w, up, up, up, gates, h, lw["w_glu"], lw["b_glu"], lw["w_up_ssm"], lw["cs"], lw["w_up_fourier"],
      lw["w_pool"], lw["pool_scale"], lw["w_up_pool"], lw["w_out"])


def _ffn_residual(h, nf_ref, wg_ref, wu_ref, wd_ref):
    hn = _rms(h, nf_ref[...]).astype(BF16)
    acc = jnp.zeros(h.shape, F32)
    for c0 in range(0, D_FF, FF_CHUNK):
        a = _dot(hn, wg_ref[:, c0 : c0 + FF_CHUNK])
        b = _dot(hn, wu_ref[:, c0 : c0 + FF_CHUNK])
        acc += _dot((jax.nn.silu(a) * b).astype(BF16), wd_ref[c0 : c0 + FF_CHUNK, :])
    return h + acc


def _ffn_next_body(h_ref, nf_ref, wg_ref, wu_ref, wd_ref, nm_ref, win_ref,
                   ho_ref, us_ref, uf_ref, up_ref, gt_ref):
    h2 = _ffn_residual(h_ref[...], nf_ref, wg_ref, wu_ref, wd_ref)
    ho_ref[...] = h2
    _emit_in_proj(_rms(h2, nm_ref[...]).astype(BF16), win_ref, us_ref, uf_ref, up_ref, gt_ref)


def _ffn_last_body(h_ref, nf_ref, wg_ref, wu_ref, wd_ref, nl_ref, o_ref):
    h2 = _ffn_residual(h_ref[...], nf_ref, wg_ref, wu_ref, wd_ref)
    o_ref[...] = _rms(h2, nl_ref[...])


def _ffn_specs(tm):
    return [
        pl.BlockSpec((tm, D_MODEL), lambda i: (i, 0)),
        _const_spec((1, D_MODEL)),
        _const_spec((D_MODEL, D_FF)),
        _const_spec((D_MODEL, D_FF)),
        _const_spec((D_FF, D_MODEL)),
        _const_spec((1, D_MODEL)),
    ]


def _ffn_next(h, lw, norm_next, w_in_next, tm):
    n = h.shape[0]
    shapes, specs = _in_proj_out(n, tm)
    return pl.pallas_call(
        _ffn_next_body,
        grid=(n // tm,),
        in_specs=_ffn_specs(tm) + [_const_spec((D_MODEL, D_MIX + 3 * D_MODEL))],
        out_specs=(pl.BlockSpec((tm, D_MODEL), lambda i: (i, 0)),) + specs,
        out_shape=(jax.ShapeDtypeStruct((n, D_MODEL), F32),) + shapes,
        compiler_params=_params(1),
        name="ffn_next",
    )(h, lw["norm_ffn"], lw["w_ff_gate"], lw["w_ff_up"], lw["w_ff_down"], norm_next, w_in_next)


def _ffn_last(h, lw, norm_final, tm):
    n = h.shape[0]
    return pl.pallas_call(
        _ffn_last_body,
        grid=(n // tm,),
        in_specs=_ffn_specs(tm),
        out_specs=pl.BlockSpec((tm, D_MODEL), lambda i: (i, 0)),
        out_shape=jax.ShapeDtypeStruct((n, D_MODEL), F32),
        compiler_params=_params(1),
        name="ffn_last",
    )(h, lw["norm_ffn"], lw["w_ff_gate"], lw["w_ff_up"], lw["w_ff_down"], norm_final)


def _trunk(x, layers, norm_mix, w_in, norm_final, cs):
    bsz, seq, _ = x.shape
    tm = TOKEN_TILE
    n_seg, n_chunks = _s5_layout(bsz, seq)
    f1cat, e2 = _dft_tables(seq)
    h = x.reshape(bsz * seq, D_MODEL)
    us, uf, up, gates = _in_proj(h, norm_mix[0], w_in[0], tm)
    for i, lw in enumerate(layers):
        y = _s5_mix(_s5_to_chunks(us, bsz, seq), lw["s5_p"], lw["s5_m"], lw["s5_q"], lw["s5_a"], n_chunks, n_seg)
        y = _s5_from_chunks(y, bsz, seq)
        w = _fourier_mix(uf, f1cat, e2, bsz, seq)
        h = _merge(y, w, up, gates, h, dict(lw, cs=cs), bsz, seq, tm)
        if i + 1 < DEPTH:
            h, us, uf, up, gates = _ffn_next(h, lw, norm_mix[i + 1], w_in[i + 1], tm)
        else:
            h = _ffn_last(h, lw, norm_final, tm)
    return h.reshape(bsz, seq, D_MODEL)


def kernel(x_prompt, x_sample, norm_mix, w_in, ssm_lam_re, ssm_lam_im, ssm_log_step, ssm_b_re, ssm_b_im,
           ssm_c_re, ssm_c_im, ssm_d, w_glu, b_glu, w_up_ssm, w_up_fourier, w_pool, pool_scale, w_up_pool,
           w_out, norm_ffn, w_ff_gate, w_ff_up, w_ff_down, norm_final):
    s5_p, s5_m, s5_q, s5_a = _s5_assemble(
        *_s5_prep(ssm_lam_re, ssm_lam_im, ssm_log_step, ssm_b_re, ssm_b_im, ssm_c_re, ssm_c_im, ssm_d)
    )
    pool_eye = jnp.eye(D_POOL // POOL_GROUP, dtype=bool)[:, None, :, None]
    layers = []
    for i in range(DEPTH):
        w_pool_blk = jnp.where(pool_eye, w_pool[i][:, :, None, :], 0.0).reshape(D_POOL, D_POOL)
        layers.append(dict(
            s5_p=s5_p[i], s5_m=s5_m[i], s5_q=s5_q[i], s5_a=s5_a[i],
            w_glu=w_glu[i].astype(BF16), b_glu=b_glu[i].reshape(1, D_SSM),
            w_up_ssm=w_up_ssm[i].astype(BF16), w_up_fourier=w_up_fourier[i].astype(BF16),
            w_pool=w_pool_blk.astype(BF16), pool_scale=pool_scale[i].reshape(1, D_POOL),
            w_up_pool=w_up_pool[i].astype(BF16), w_out=w_out[i].astype(BF16),
            norm_ffn=norm_ffn[i].reshape(1, D_MODEL), w_ff_gate=w_ff_gate[i].astype(BF16),
            w_ff_up=w_ff_up[i].astype(BF16), w_ff_down=w_ff_down[i].astype(BF16),
        ))
    norm_mix = norm_mix.reshape(DEPTH, 1, D_MODEL)
    w_in = w_in.astype(BF16)
    norm_final = norm_final.reshape(1, D_MODEL)
    cs = _channel_dft_table()
    y_prompt = _trunk(x_prompt, layers, norm_mix, w_in, norm_final, cs)
    y_sample = _trunk(x_sample, layers, norm_mix, w_in, norm_final, cs)
    return (y_prompt, y_sample)
```

```python
import functools
import math

import jax
import jax.numpy as jnp
import numpy as np
from jax import lax
from jax.experimental import pallas as pl
from jax.experimental.pallas import tpu as pltpu

F32 = jnp.float32
BF16 = jnp.bfloat16

D_MODEL = 1024
DEPTH = 4
D_SSM = 512
SSM_GROUP = 16
N_SSM_GROUPS = D_SSM // SSM_GROUP
SSM_STATE = 64
D_FOURIER = 256
FOURIER_GROUP = 64
D_POOL = 256
POOL_GROUP = 64
D_MIX = D_SSM + D_FOURIER + D_POOL
D_FF = 2816
EPS = 1e-6

CHUNK = 16
SCAN_ROWS = 16
PREP_GROUPS = 8
FF_CHUNK = 256
TOKEN_TILE = 512
VMEM_LIMIT = 58 * 1024 * 1024


def _dot(a, b):
    return jnp.dot(a, b, preferred_element_type=F32)


def _rms(x, g):
    return x * lax.rsqrt(jnp.mean(x * x, axis=-1, keepdims=True) + EPS) * g


def _const_spec(shape):
    nd = len(shape)
    return pl.BlockSpec(shape, lambda *_: (0,) * nd, pipeline_mode=pl.Buffered(1))


def _params(n_axes):
    return pltpu.CompilerParams(
        dimension_semantics=("arbitrary",) * n_axes, vmem_limit_bytes=VMEM_LIMIT
    )


def _block_transpose(tiles):
    tiles = list(tiles)
    block = lax.broadcasted_iota(jnp.int32, tiles[0].shape, 1) // SSM_GROUP
    for d in (4, 2, 1):
        keep = (block & d) == 0
        for s in range(8):
            if s & d:
                continue
            lo, hi = tiles[s], tiles[s + d]
            tiles[s] = jnp.where(keep, lo, pltpu.roll(hi, SSM_GROUP * d, 1))
            tiles[s + d] = jnp.where(keep, pltpu.roll(lo, 128 - SSM_GROUP * d, 1), hi)
    return tiles


def _emit_in_proj(xb, w_ref, x_ref, uf_ref, up_ref, gt_ref, us_scr):
    n_rows = us_scr.shape[1] // CHUNK
    us = _dot(xb, w_ref[:, 0:D_SSM])
    for o in range(D_SSM // 128):
        us_scr[o] = us[:, 128 * o : 128 * (o + 1)]
        for half in range(2):
            tok = [us_scr[o, pl.ds(8 * half + s, n_rows, stride=CHUNK), :] for s in range(8)]
            for j, t in enumerate(_block_transpose(tok)):
                x_ref[8 * o + j, :, 128 * half : 128 * (half + 1)] = t.astype(BF16)
    uf_ref[...] = _dot(xb, w_ref[:, D_SSM : D_SSM + D_FOURIER]).astype(BF16)
    up_ref[...] = _dot(xb, w_ref[:, D_SSM + D_FOURIER : D_MIX]).astype(BF16)
    for j in range(3):
        lo = D_MIX + j * D_MODEL
        z = _dot(xb, w_ref[:, lo : lo + D_MODEL])
        gt_ref[:, j * D_MODEL : (j + 1) * D_MODEL] = jax.nn.sigmoid(z).astype(BF16)


def _in_proj_body(h_ref, g_ref, w_ref, x_ref, uf_ref, up_ref, gt_ref, us_scr):
    xb = _rms(h_ref[...], g_ref[...]).astype(BF16)
    _emit_in_proj(xb, w_ref, x_ref, uf_ref, up_ref, gt_ref, us_scr)


def _in_proj_out(n, tm):
    shapes = (
        jax.ShapeDtypeStruct((N_SSM_GROUPS, n // CHUNK, CHUNK * SSM_GROUP), BF16),
        jax.ShapeDtypeStruct((n, D_FOURIER), BF16),
        jax.ShapeDtypeStruct((n, D_POOL), BF16),
        jax.ShapeDtypeStruct((n, 3 * D_MODEL), BF16),
    )
    specs = (
        pl.BlockSpec((N_SSM_GROUPS, tm // CHUNK, CHUNK * SSM_GROUP), lambda i: (0, i, 0)),
        pl.BlockSpec((tm, D_FOURIER), lambda i: (i, 0)),
        pl.BlockSpec((tm, D_POOL), lambda i: (i, 0)),
        pl.BlockSpec((tm, 3 * D_MODEL), lambda i: (i, 0)),
    )
    return shapes, specs


def _in_proj(h, g, w_in, tm):
    n = h.shape[0]
    shapes, specs = _in_proj_out(n, tm)
    return pl.pallas_call(
        _in_proj_body,
        grid=(n // tm,),
        in_specs=[
            pl.BlockSpec((tm, D_MODEL), lambda i: (i, 0)),
            _const_spec((1, D_MODEL)),
            _const_spec((D_MODEL, D_MIX + 3 * D_MODEL)),
        ],
        out_specs=specs,
        out_shape=shapes,
        scratch_shapes=[pltpu.VMEM((D_SSM // 128, tm, 128), F32)],
        compiler_params=_params(1),
        name="in_proj",
    )(h, g, w_in)


def _s5_prep_body(lre_ref, lim_ref, ls_ref, btr_ref, bti_ref, cr_ref, ci_ref, d_ref,
                  w_ref, qt_ref, km_ref, a_ref):
    is_fwd = (pl.program_id(1) == 0).astype(F32)
    lre = jnp.minimum(lre_ref[0, 0], -1e-4)
    lim = lim_ref[0, 0]
    step = jnp.exp(ls_ref[0, 0])
    mag = jnp.exp(lre * step)
    lbr = mag * jnp.cos(lim * step)
    lbi = mag * jnp.sin(lim * step)
    den = lre * lre + lim * lim
    nr = lbr - 1.0
    kr = (nr * lre + lbi * lim) / den
    ki = (lbi * lre - nr * lim) / den
    btr = btr_ref[0]
    bti = bti_ref[0]
    bbr = kr * btr - ki * bti
    bbi = kr * bti + ki * btr
    cr = cr_ref[0, 0]
    ci = ci_ref[0, 0]
    gb = cr.shape[0]
    eye = (
        lax.broadcasted_iota(jnp.int32, (gb, SSM_GROUP, SSM_GROUP), 1)
        == lax.broadcasted_iota(jnp.int32, (gb, SSM_GROUP, SSM_GROUP), 2)
    ).astype(F32)
    d_diag = eye * d_ref[0] * is_fwd
    pr = jnp.ones_like(lbr)
    pi = jnp.zeros_like(lbr)
    for j in range(CHUNK):
        rows = slice(j * SSM_GROUP, (j + 1) * SSM_GROUP)
        wr = pr * bbr - pi * bbi
        wi = pr * bbi + pi * bbr
        w_ref[0, 0, 0, :, rows, :] = wr
        w_ref[0, 0, 1, :, rows, :] = wi
        km = jnp.einsum("gap,gcp->gac", wr, cr, precision=lax.Precision.HIGHEST,
                        preferred_element_type=F32)
        km = km - jnp.einsum("gap,gcp->gac", wi, ci, precision=lax.Precision.HIGHEST,
                             preferred_element_type=F32)
        if j == 0:
            km = km + d_diag
        km_ref[0, 0, :, rows, :] = km
        pr, pi = pr * lbr - pi * lbi, pr * lbi + pi * lbr
        qt_ref[0, 0, 0, :, rows, :] = cr * pr - ci * pi
        qt_ref[0, 0, 1, :, rows, :] = -(cr * pi + ci * pr)
    a_ref[0, 0, 0] = pr
    a_ref[0, 0, 1] = pi


def _s5_prep(lam_re, lam_im, log_step, b_re, b_im, c_re, c_im, d_skip):
    g, p, gb = N_SSM_GROUPS, SSM_STATE, PREP_GROUPS
    lre = lam_re.reshape(DEPTH, 2, g, 1, p)
    lim = lam_im.reshape(DEPTH, 2, g, 1, p)
    ls = jnp.broadcast_to(log_step.reshape(DEPTH, 2, g, 1, 1), (DEPTH, 2, g, 1, p))
    btr = jnp.swapaxes(b_re, 2, 3)
    bti = jnp.swapaxes(b_im, 2, 3)
    d4 = d_skip.reshape(DEPTH, g, 1, SSM_GROUP)
    pole_spec = pl.BlockSpec((1, 1, gb, 1, p), lambda i, d, j: (i, d, j, 0, 0))
    b_spec = pl.BlockSpec((1, gb, SSM_GROUP, p), lambda i, d, j: (i, j, 0, 0))
    c_spec = pl.BlockSpec((1, 1, gb, SSM_GROUP, p), lambda i, d, j: (i, d, j, 0, 0))
    rows = CHUNK * SSM_GROUP
    return pl.pallas_call(
        _s5_prep_body,
        grid=(DEPTH, 2, g // gb),
        in_specs=[pole_spec, pole_spec, pole_spec, b_spec, b_spec, c_spec, c_spec,
                  pl.BlockSpec((1, gb, 1, SSM_GROUP), lambda i, d, j: (i, j, 0, 0))],
        out_specs=(
            pl.BlockSpec((1, 1, 2, gb, rows, p), lambda i, d, j: (i, d, 0, j, 0, 0)),
            pl.BlockSpec((1, 1, 2, gb, rows, p), lambda i, d, j: (i, d, 0, j, 0, 0)),
            pl.BlockSpec((1, 1, gb, rows, SSM_GROUP), lambda i, d, j: (i, d, j, 0, 0)),
            pl.BlockSpec((1, 1, 2, gb, 1, p), lambda i, d, j: (i, d, 0, j, 0, 0)),
        ),
        out_shape=(
            jax.ShapeDtypeStruct((DEPTH, 2, 2, g, rows, p), F32),
            jax.ShapeDtypeStruct((DEPTH, 2, 2, g, rows, p), F32),
            jax.ShapeDtypeStruct((DEPTH, 2, g, rows, SSM_GROUP), F32),
            jax.ShapeDtypeStruct((DEPTH, 2, 2, g, 1, p), F32),
        ),
        compiler_params=_params(3),
        name="s5_prep",
    )(lre, lim, ls, btr, bti, c_re, c_im, d4)


def _s5_assemble(w, qt, km, a16):
    g, p, t, c = N_SSM_GROUPS, SSM_STATE, CHUNK, SSM_GROUP
    w = w.reshape(DEPTH, 2, 2, g, t, c, p)
    qt = qt.reshape(DEPTH, 2, 2, g, t, c, p)
    pf = w[:, 0, :, :, ::-1]
    pb = w[:, 1]
    pcomp = jnp.stack([pf[:, 0], pf[:, 1], pb[:, 0], pb[:, 1]], axis=1)
    pcomp = pcomp.reshape(DEPTH, 4, g // 2, 2, t * c, p)
    pair_eye = jnp.eye(2, dtype=bool)[None, None, None, :, None, :, None]
    pmat = jnp.where(pair_eye, pcomp[:, :, :, :, :, None, :], 0.0)
    pmat = jnp.transpose(pmat, (0, 2, 3, 4, 1, 5, 6)).reshape(DEPTH, g // 2, 2 * t * c, 4 * 2 * p)
    qf = qt[:, 0]
    qb = qt[:, 1, :, :, ::-1]
    qcomp = jnp.stack([qf[:, 0], qf[:, 1], qb[:, 0], qb[:, 1]], axis=1)
    qcomp = qcomp.reshape(DEPTH, 4, g // 2, 2, t * c, p)
    qmat = jnp.where(pair_eye, qcomp[:, :, :, :, :, None, :], 0.0)
    qmat = jnp.transpose(qmat, (0, 2, 3, 1, 5, 6, 4)).reshape(DEPTH, g, 4 * 2 * p, t * c)
    km = km.reshape(DEPTH, 2, g, t, c, c)
    s_idx = np.arange(t)[:, None]
    t_idx = np.arange(t)[None, :]
    mf = jnp.where((t_idx >= s_idx)[None, None, :, :, None, None],
                   km[:, 0][:, :, np.clip(t_idx - s_idx, 0, t - 1)], 0.0)
    mb = jnp.where((t_idx <= s_idx)[None, None, :, :, None, None],
                   km[:, 1][:, :, np.clip(s_idx - t_idx, 0, t - 1)], 0.0)
    mmat = jnp.stack([mf, mb], axis=2)
    mmat = jnp.transpose(mmat, (0, 1, 2, 3, 5, 4, 6)).reshape(DEPTH, g, 2, t * c, t * c)
    acoef = a16.reshape(DEPTH, 4, g // 2, 2 * p)
    acoef = jnp.transpose(acoef, (0, 2, 1, 3))
    acoef = jnp.concatenate([acoef, jnp.zeros_like(acoef)], axis=2)
    return pmat.astype(BF16), mmat.astype(BF16), qmat.astype(BF16), acoef


def _cmul(ar, ai, br, bi):
    return ar * br - ai * bi, ar * bi + ai * br


def _cpow(ar, ai, n):
    rr, ri = jnp.ones_like(ar), jnp.zeros_like(ar)
    while n:
        if n & 1:
            rr, ri = _cmul(rr, ri, ar, ai)
        ar, ai = _cmul(ar, ai, ar, ai)
        n >>= 1
    return rr, ri


def _s5_body(x_ref, p_ref, m_ref, q_ref, a_ref, y_ref, sh_ref, *, n_chunks, n_seg, row_block):
    rows = n_chunks * SCAN_ROWS
    lanes = 2 * SSM_STATE
    for r0 in range(0, rows, row_block):
        rs = slice(r0, r0 + row_block)
        s = _dot(x_ref[0, rs, :], p_ref[0, 0:256, :]) + _dot(x_ref[1, rs, :], p_ref[0, 256:512, :])
        for j in range(4):
            sh_ref[j, rs, :] = s[:, j * lanes : (j + 1) * lanes]

    def coef(i):
        return jnp.broadcast_to(a_ref[0, i : i + 1, :], (SCAN_ROWS, lanes))

    afr, afi, abr, abi = coef(0), coef(1), coef(2), coef(3)
    zero = jnp.zeros((SCAN_ROWS, lanes), F32)

    def tile(k, j):
        return (j, pl.ds(k, SCAN_ROWS, stride=n_chunks), slice(None))

    def scan(k, carry):
        fr, fi, br, bi = carry
        kb = n_chunks - 1 - k
        sfr, sfi = sh_ref[tile(k, 0)], sh_ref[tile(k, 1)]
        sbr, sbi = sh_ref[tile(kb, 2)], sh_ref[tile(kb, 3)]
        sh_ref[tile(k, 0)] = fr
        sh_ref[tile(k, 1)] = fi
        sh_ref[tile(kb, 2)] = br
        sh_ref[tile(kb, 3)] = bi
        nfr, nfi = _cmul(afr, afi, fr, fi)
        nbr, nbi = _cmul(abr, abi, br, bi)
        return nfr + sfr, nfi + sfi, nbr + sbr, nbi + sbi

    fr, fi, br, bi = lax.fori_loop(0, n_chunks, scan, (zero, zero, zero, zero))

    if n_seg > 1:
        seg = lax.broadcasted_iota(jnp.int32, (SCAN_ROWS, lanes), 0) % n_seg
        akfr, akfi = _cpow(afr, afi, n_chunks)
        akbr, akbi = _cpow(abr, abi, n_chunks)
        cfr, cfi, cbr, cbi = zero, zero, zero, zero
        for _ in range(n_seg - 1):
            tr, ti = _cmul(akfr, akfi, cfr, cfi)
            cfr = jnp.where(seg == 0, 0.0, pltpu.roll(tr + fr, 1, 0))
            cfi = jnp.where(seg == 0, 0.0, pltpu.roll(ti + fi, 1, 0))
            tr, ti = _cmul(akbr, akbi, cbr, cbi)
            cbr = jnp.where(seg == n_seg - 1, 0.0, pltpu.roll(tr + br, SCAN_ROWS - 1, 0))
            cbi = jnp.where(seg == n_seg - 1, 0.0, pltpu.roll(ti + bi, SCAN_ROWS - 1, 0))

        def fix(k, carry):
            pfr, pfi, pbr, pbi = carry
            kb = n_chunks - 1 - k
            dr, di = _cmul(pfr, pfi, cfr, cfi)
            sh_ref[tile(k, 0)] += dr
            sh_ref[tile(k, 1)] += di
            dr, di = _cmul(pbr, pbi, cbr, cbi)
            sh_ref[tile(kb, 2)] += dr
            sh_ref[tile(kb, 3)] += di
            return _cmul(afr, afi, pfr, pfi) + _cmul(abr, abi, pbr, pbi)

        one = jnp.ones((SCAN_ROWS, lanes), F32)
        lax.fori_loop(0, n_chunks, fix, (one, zero, one, zero))

    for r0 in range(0, rows, row_block):
        rs = slice(r0, r0 + row_block)
        hb = jnp.concatenate([sh_ref[j, rs, :] for j in range(4)], axis=1).astype(BF16)
        for gi in range(2):
            x = x_ref[gi, rs, :]
            y_ref[gi, rs, :] = _dot(x, m_ref[gi, 0]) + _dot(x, m_ref[gi, 1]) + _dot(hb, q_ref[gi])


def _s5_mix(x, pmat, mmat, qmat, acoef, n_chunks, n_seg):
    g = N_SSM_GROUPS
    rows = n_chunks * SCAN_ROWS
    width = CHUNK * SSM_GROUP
    body = functools.partial(_s5_body, n_chunks=n_chunks, n_seg=n_seg, row_block=min(rows, 1024))
    return pl.pallas_call(
        body,
        grid=(g // 2,),
        in_specs=[
            pl.BlockSpec((2, rows, width), lambda i: (i, 0, 0)),
            pl.BlockSpec((1, 2 * width, 8 * SSM_STATE), lambda i: (i, 0, 0)),
            pl.BlockSpec((2, 2, width, width), lambda i: (i, 0, 0, 0)),
            pl.BlockSpec((2, 8 * SSM_STATE, width), lambda i: (i, 0, 0)),
            pl.BlockSpec((1, 8, 2 * SSM_STATE), lambda i: (i, 0, 0)),
        ],
        out_specs=pl.BlockSpec((2, rows, width), lambda i: (i, 0, 0)),
        out_shape=jax.ShapeDtypeStruct((g, rows, width), F32),
        scratch_shapes=[pltpu.VMEM((4, rows, 2 * SSM_STATE), F32)],
        compiler_params=_params(1),
        name="s5_mix",
    )(x, pmat, mmat, qmat, acoef)


def _s5_layout(bsz, seq):
    n_seg = SCAN_ROWS // bsz
    n_chunks = seq // (CHUNK * n_seg)
    assert bsz * n_seg == SCAN_ROWS and n_chunks * n_seg * CHUNK == seq
    return n_seg, n_chunks


def _dft_split(seq):
    l1 = 1 << (int(math.log2(seq)) // 2)
    return l1, seq // l1


def _dft_tables(seq):
    l1, l2 = _dft_split(seq)
    k1 = (jnp.arange(l1)[:, None] * jnp.arange(l1)[None, :]) % l1
    th1 = k1.astype(F32) * (2.0 * math.pi / l1)
    f1cat = jnp.concatenate([jnp.cos(th1), -jnp.sin(th1)], axis=0)
    f = jnp.arange(l1)[:, None, None] + l1 * jnp.arange(l2)[None, :, None]
    k2 = (f * jnp.arange(l2)[None, None, :]) % seq
    th2 = k2.astype(F32) * (2.0 * math.pi / seq)
    c2, s2 = jnp.cos(th2), jnp.sin(th2)
    e2 = jnp.concatenate(
        [jnp.concatenate([c2, s2], axis=2), jnp.concatenate([-s2, c2], axis=2)], axis=1
    )
    return f1cat.astype(BF16), e2.astype(BF16)


def _channel_dft_table():
    n = FOURIER_GROUP
    th = ((jnp.arange(n)[:, None] * jnp.arange(n)[None, :]) % n).astype(F32) * (2.0 * math.pi / n)
    eye = jnp.eye(D_FOURIER // n, dtype=F32)
    return jnp.concatenate([jnp.kron(eye, jnp.cos(th)), jnp.kron(eye, jnp.sin(th))], axis=0).astype(BF16)


def _dft1_body(f_ref, v_ref, a_ref):
    a_ref[0] = _dot(f_ref[...], v_ref[0]).astype(BF16)


def _dft2_body(e_ref, a_ref, o_ref, *, n_f1):
    for i in range(n_f1):
        o_ref[0, :, i * D_FOURIER : (i + 1) * D_FOURIER] = _dot(e_ref[i], a_ref[0, i]).astype(BF16)


def _fourier_mix(u, f1cat, e2, bsz, seq):
    l1, l2 = _dft_split(seq)
    cols = l2 * D_FOURIER
    tn = min(cols, 4096)
    v = u.reshape(bsz, l1, cols)
    a = pl.pallas_call(
        _dft1_body,
        grid=(bsz, cols // tn),
        in_specs=[_const_spec((2 * l1, l1)), pl.BlockSpec((1, l1, tn), lambda b, j: (b, 0, j))],
        out_specs=pl.BlockSpec((1, 2 * l1, tn), lambda b, j: (b, 0, j)),
        out_shape=jax.ShapeDtypeStruct((bsz, 2 * l1, cols), BF16),
        compiler_params=_params(2),
        name="dft_stage1",
    )(f1cat, v)
    a = a.reshape(bsz, 2, l1, l2, D_FOURIER)
    a = jnp.transpose(a, (0, 2, 1, 3, 4)).reshape(bsz, l1, 2 * l2, D_FOURIER)
    n_f1 = 8
    w = pl.pallas_call(
        functools.partial(_dft2_body, n_f1=n_f1),
        grid=(l1 // n_f1, bsz),
        in_specs=[
            pl.BlockSpec((n_f1, 2 * l2, 2 * l2), lambda j, b: (j, 0, 0)),
            pl.BlockSpec((1, n_f1, 2 * l2, D_FOURIER), lambda j, b: (b, j, 0, 0)),
        ],
        out_specs=pl.BlockSpec((1, 2 * l2, n_f1 * D_FOURIER), lambda j, b: (b, 0, j)),
        out_shape=jax.ShapeDtypeStruct((bsz, 2 * l2, l1 * D_FOURIER), BF16),
        compiler_params=_params(2),
        name="dft_stage2",
    )(e2, a)
    return w.reshape(bsz, 2, seq, D_FOURIER)


def _pool_windows(up, prev, nxt, t0, seq):
    tm = up.shape[0]
    ext_rows = tm + 16
    ext = jnp.concatenate([prev, up, nxt], axis=0)
    s2 = ext + pltpu.roll(ext, 1, 0)
    s4 = s2 + pltpu.roll(s2, 2, 0)
    s8 = s4 + pltpu.roll(s4, 4, 0)
    s16 = s8 + pltpu.roll(s8, 8, 0)
    p2 = s2[8 : tm + 8]
    p4 = pltpu.roll(s4, ext_rows - 1, 0)[8 : tm + 8]
    p8 = pltpu.roll(s8, ext_rows - 3, 0)[8 : tm + 8]
    p16 = pltpu.roll(s16, ext_rows - 7, 0)[8 : tm + 8]
    grp = lax.broadcasted_iota(jnp.int32, (tm, D_POOL), 1) // POOL_GROUP
    t = t0 + lax.broadcasted_iota(jnp.int32, (tm, D_POOL), 0)
    half = jnp.left_shift(1, grp)
    cnt = jnp.minimum(t + half, seq) - jnp.maximum(t - half, 0)
    pooled = jnp.where(grp == 0, p2, jnp.where(grp == 1, p4, jnp.where(grp == 2, p8, p16)))
    return pooled / cnt.astype(F32) - up


def _merge_body(y_ref, w_ref, up_ref, upp_ref, upn_ref, gt_ref, h_ref,
                wglu_ref, bglu_ref, wups_ref, cs_ref, wupf_ref, wpool_ref, psc_ref, wupp_ref, wout_ref,
                o_ref, y_scr, *, tm, seq, four_scale):
    t0 = (pl.program_id(0) % (seq // tm)) * tm
    for o in range(D_SSM // 128):
        for half in range(2):
            grp = [y_ref[8 * o + j, :, 128 * half : 128 * (half + 1)] for j in range(8)]
            for s, t in enumerate(_block_transpose(grp)):
                y_scr[o, pl.ds(8 * half + s, tm // CHUNK, stride=CHUNK), :] = t
    g = jax.nn.gelu(jnp.concatenate([y_scr[o] for o in range(D_SSM // 128)], axis=1))
    s = g * jax.nn.sigmoid(_dot(g.astype(BF16), wglu_ref[...]) + bglu_ref[...])
    merged = gt_ref[:, 0:D_MODEL].astype(F32) * _dot(s.astype(BF16), wups_ref[...])
    four = _dot(w_ref[0, 0], cs_ref[0:D_FOURIER, :]) + _dot(w_ref[0, 1], cs_ref[D_FOURIER:, :])
    four = four * four_scale
    merged += gt_ref[:, D_MODEL : 2 * D_MODEL].astype(F32) * _dot(four.astype(BF16), wupf_ref[...])
    up = up_ref[...].astype(F32)
    prev = jnp.where(t0 == 0, 0.0, upp_ref[8:16, :].astype(F32))
    nxt = jnp.where(t0 + tm == seq, 0.0, upn_ref[0:8, :].astype(F32))
    pw = _pool_windows(up, prev, nxt, t0, seq)
    pm = _dot(pw.astype(BF16), wpool_ref[...]) * psc_ref[...]
    merged += gt_ref[:, 2 * D_MODEL :].astype(F32) * _dot(pm.astype(BF16), wupp_ref[...])
    o_ref[...] = h_ref[...] + _dot(merged.astype(BF16), wout_ref[...])


def _merge(y, w, up, gates, h, lw, bsz, seq, tm):
    n = bsz * seq
    per_seq = seq // tm
    halo = tm // 16
    body = functools.partial(_merge_body, tm=tm, seq=seq, four_scale=1.0 / math.sqrt(seq * FOURIER_GROUP))
    return pl.pallas_call(
        body,
        grid=(n // tm,),
        in_specs=[
            pl.BlockSpec((N_SSM_GROUPS, tm // CHUNK, CHUNK * SSM_GROUP), lambda i: (0, i, 0)),
            pl.BlockSpec((1, 2, tm, D_FOURIER), lambda i: (i // per_seq, 0, i % per_seq, 0)),
            pl.BlockSpec((tm, D_POOL), lambda i: (i, 0)),
            pl.BlockSpec((16, D_POOL), lambda i: (jnp.maximum(i * halo - 1, 0), 0)),
            pl.BlockSpec((16, D_POOL), lambda i: (jnp.minimum((i + 1) * halo, n // 16 - 1), 0)),
            pl.BlockSpec((tm, 3 * D_MODEL), lambda i: (i, 0)),
            pl.BlockSpec((tm, D_MODEL), lambda i: (i, 0)),
            _const_spec((D_SSM, D_SSM)),
            _const_spec((1, D_SSM)),
            _const_spec((D_SSM, D_MODEL)),
            _const_spec((2 * D_FOURIER, D_FOURIER)),
            _const_spec((D_FOURIER, D_MODEL)),
            _const_spec((D_POOL, D_POOL)),
            _const_spec((1, D_POOL)),
            _const_spec((D_POOL, D_MODEL)),
            _const_spec((D_MODEL, D_MODEL)),
        ],
        out_specs=pl.BlockSpec((tm, D_MODEL), lambda i: (i, 0)),
        out_shape=jax.ShapeDtypeStruct((n, D_MODEL), F32),
        scratch_shapes=[pltpu.VMEM((D_SSM // 128, tm, 128), F32)],
        compiler_params=_params(1),
        name="merge",
    )(y, w, up, up, up, gates, h, lw["w_glu"], lw["b_glu"], lw["w_up_ssm"], lw["cs"], lw["w_up_fourier"],
      lw["w_pool"], lw["pool_scale"], lw["w_up_pool"], lw["w_out"])


def _ffn_residual(h, nf_ref, wg_ref, wu_ref, wd_ref):
    hn = _rms(h, nf_ref[...]).astype(BF16)
    acc = jnp.zeros(h.shape, F32)
    for c0 in range(0, D_FF, FF_CHUNK):
        a = _dot(hn, wg_ref[:, c0 : c0 + FF_CHUNK])
        b = _dot(hn, wu_ref[:, c0 : c0 + FF_CHUNK])
        acc += _dot((jax.nn.silu(a) * b).astype(BF16), wd_ref[c0 : c0 + FF_CHUNK, :])
    return h + acc


def _ffn_next_body(h_ref, nf_ref, wg_ref, wu_ref, wd_ref, nm_ref, win_ref,
                   ho_ref, x_ref, uf_ref, up_ref, gt_ref, us_scr):
    h2 = _ffn_residual(h_ref[...], nf_ref, wg_ref, wu_ref, wd_ref)
    ho_ref[...] = h2
    _emit_in_proj(_rms(h2, nm_ref[...]).astype(BF16), win_ref, x_ref, uf_ref, up_ref, gt_ref, us_scr)


def _ffn_last_body(h_ref, nf_ref, wg_ref, wu_ref, wd_ref, nl_ref, o_ref):
    h2 = _ffn_residual(h_ref[...], nf_ref, wg_ref, wu_ref, wd_ref)
    o_ref[...] = _rms(h2, nl_ref[...])


def _ffn_specs(tm):
    return [
        pl.BlockSpec((tm, D_MODEL), lambda i: (i, 0)),
        _const_spec((1, D_MODEL)),
        _const_spec((D_MODEL, D_FF)),
        _const_spec((D_MODEL, D_FF)),
        _const_spec((D_FF, D_MODEL)),
        _const_spec((1, D_MODEL)),
    ]


def _ffn_next(h, lw, norm_next, w_in_next, tm):
    n = h.shape[0]
    shapes, specs = _in_proj_out(n, tm)
    return pl.pallas_call(
        _ffn_next_body,
        grid=(n // tm,),
        in_specs=_ffn_specs(tm) + [_const_spec((D_MODEL, D_MIX + 3 * D_MODEL))],
        out_specs=(pl.BlockSpec((tm, D_MODEL), lambda i: (i, 0)),) + specs,
        out_shape=(jax.ShapeDtypeStruct((n, D_MODEL), F32),) + shapes,
        scratch_shapes=[pltpu.VMEM((D_SSM // 128, tm, 128), F32)],
        compiler_params=_params(1),
        name="ffn_next",
    )(h, lw["norm_ffn"], lw["w_ff_gate"], lw["w_ff_up"], lw["w_ff_down"], norm_next, w_in_next)


def _ffn_last(h, lw, norm_final, tm):
    n = h.shape[0]
    return pl.pallas_call(
        _ffn_last_body,
        grid=(n // tm,),
        in_specs=_ffn_specs(tm),
        out_specs=pl.BlockSpec((tm, D_MODEL), lambda i: (i, 0)),
        out_shape=jax.ShapeDtypeStruct((n, D_MODEL), F32),
        compiler_params=_params(1),
        name="ffn_last",
    )(h, lw["norm_ffn"], lw["w_ff_gate"], lw["w_ff_up"], lw["w_ff_down"], norm_final)


def _trunk(x, layers, norm_mix, w_in, norm_final, cs):
    bsz, seq, _ = x.shape
    tm = TOKEN_TILE
    n_seg, n_chunks = _s5_layout(bsz, seq)
    f1cat, e2 = _dft_tables(seq)
    h = x.reshape(bsz * seq, D_MODEL)
    xs, uf, up, gates = _in_proj(h, norm_mix[0], w_in[0], tm)
    for i, lw in enumerate(layers):
        y = _s5_mix(xs, lw["s5_p"], lw["s5_m"], lw["s5_q"], lw["s5_a"], n_chunks, n_seg)
        w = _fourier_mix(uf, f1cat, e2, bsz, seq)
        h = _merge(y, w, up, gates, h, dict(lw, cs=cs), bsz, seq, tm)
        if i + 1 < DEPTH:
            h, xs, uf, up, gates = _ffn_next(h, lw, norm_mix[i + 1], w_in[i + 1], tm)
        else:
            h = _ffn_last(h, lw, norm_final, tm)
    return h.reshape(bsz, seq, D_MODEL)


def kernel(x_prompt, x_sample, norm_mix, w_in, ssm_lam_re, ssm_lam_im, ssm_log_step, ssm_b_re, ssm_b_im,
           ssm_c_re, ssm_c_im, ssm_d, w_glu, b_glu, w_up_ssm, w_up_fourier, w_pool, pool_scale, w_up_pool,
           w_out, norm_ffn, w_ff_gate, w_ff_up, w_ff_down, norm_final):
    s5_p, s5_m, s5_q, s5_a = _s5_assemble(
        *_s5_prep(ssm_lam_re, ssm_lam_im, ssm_log_step, ssm_b_re, ssm_b_im, ssm_c_re, ssm_c_im, ssm_d)
    )
    pool_eye = jnp.eye(D_POOL // POOL_GROUP, dtype=bool)[:, None, :, None]
    layers = []
    for i in range(DEPTH):
        w_pool_blk = jnp.where(pool_eye, w_pool[i][:, :, None, :], 0.0).reshape(D_POOL, D_POOL)
        layers.append(dict(
            s5_p=s5_p[i], s5_m=s5_m[i], s5_q=s5_q[i], s5_a=s5_a[i],
            w_glu=w_glu[i].astype(BF16), b_glu=b_glu[i].reshape(1, D_SSM),
            w_up_ssm=w_up_ssm[i].astype(BF16), w_up_fourier=w_up_fourier[i].astype(BF16),
            w_pool=w_pool_blk.astype(BF16), pool_scale=pool_scale[i].reshape(1, D_POOL),
            w_up_pool=w_up_pool[i].astype(BF16), w_out=w_out[i].astype(BF16),
            norm_ffn=norm_ffn[i].reshape(1, D_MODEL), w_ff_gate=w_ff_gate[i].astype(BF16),
            w_ff_up=w_ff_up[i].astype(BF16), w_ff_down=w_ff_down[i].astype(BF16),
        ))
    norm_mix = norm_mix.reshape(DEPTH, 1, D_MODEL)
    w_in = w_in.astype(BF16)
    norm_final = norm_final.reshape(1, D_MODEL)
    cs = _channel_dft_table()
    y_prompt = _trunk(x_prompt, layers, norm_mix, w_in, norm_final, cs)
    y_sample = _trunk(x_sample, layers, norm_mix, w_in, norm_final, cs)
    return (y_prompt, y_sample)
```

```python
import functools
import math

import jax
import jax.numpy as jnp
import numpy as np
from jax import lax
from jax.experimental import pallas as pl
from jax.experimental.pallas import tpu as pltpu

F32 = jnp.float32
BF16 = jnp.bfloat16

D_MODEL = 1024
DEPTH = 4
D_SSM = 512
SSM_GROUP = 16
N_SSM_GROUPS = D_SSM // SSM_GROUP
SSM_STATE = 64
D_FOURIER = 256
FOURIER_GROUP = 64
D_POOL = 256
POOL_GROUP = 64
D_MIX = D_SSM + D_FOURIER + D_POOL
D_FF = 2816
EPS = 1e-6

CHUNK = 16
SCAN_ROWS = 16
SCAN_PAD = 4
PREP_GROUPS = 8
FF_CHUNK = 256
TOKEN_TILE = 512
VMEM_LIMIT = 58 * 1024 * 1024


def _dot(a, b):
    return jnp.dot(a, b, preferred_element_type=F32)


def _rms(x, g):
    return x * lax.rsqrt(jnp.mean(x * x, axis=-1, keepdims=True) + EPS) * g


def _const_spec(shape):
    nd = len(shape)
    return pl.BlockSpec(shape, lambda *_: (0,) * nd, pipeline_mode=pl.Buffered(1))


def _params(n_axes):
    return pltpu.CompilerParams(
        dimension_semantics=("arbitrary",) * n_axes, vmem_limit_bytes=VMEM_LIMIT
    )


def _block_transpose(tiles):
    tiles = list(tiles)
    block = lax.broadcasted_iota(jnp.int32, tiles[0].shape, 1) // SSM_GROUP
    for d in (4, 2, 1):
        keep = (block & d) == 0
        for s in range(8):
            if s & d:
                continue
            lo, hi = tiles[s], tiles[s + d]
            tiles[s] = jnp.where(keep, lo, pltpu.roll(hi, SSM_GROUP * d, 1))
            tiles[s + d] = jnp.where(keep, pltpu.roll(lo, 128 - SSM_GROUP * d, 1), hi)
    return tiles


def _emit_in_proj(xb, w_ref, x_ref, uf_ref, up_ref, gt_ref, us_scr):
    n_rows = us_scr.shape[1] // CHUNK
    us = _dot(xb, w_ref[:, 0:D_SSM])
    for o in range(D_SSM // 128):
        us_scr[o] = us[:, 128 * o : 128 * (o + 1)]
        for half in range(2):
            tok = [us_scr[o, pl.ds(8 * half + s, n_rows, stride=CHUNK), :] for s in range(8)]
            for j, t in enumerate(_block_transpose(tok)):
                x_ref[8 * o + j, :, 128 * half : 128 * (half + 1)] = t.astype(BF16)
    uf_ref[...] = _dot(xb, w_ref[:, D_SSM : D_SSM + D_FOURIER]).astype(BF16)
    up_ref[...] = _dot(xb, w_ref[:, D_SSM + D_FOURIER : D_MIX]).astype(BF16)
    for j in range(3):
        lo = D_MIX + j * D_MODEL
        z = _dot(xb, w_ref[:, lo : lo + D_MODEL])
        gt_ref[:, j * D_MODEL : (j + 1) * D_MODEL] = jax.nn.sigmoid(z).astype(BF16)


def _in_proj_body(h_ref, g_ref, w_ref, x_ref, uf_ref, up_ref, gt_ref, us_scr):
    xb = _rms(h_ref[...], g_ref[...]).astype(BF16)
    _emit_in_proj(xb, w_ref, x_ref, uf_ref, up_ref, gt_ref, us_scr)


def _in_proj_out(n, tm):
    shapes = (
        jax.ShapeDtypeStruct((N_SSM_GROUPS, n // CHUNK, CHUNK * SSM_GROUP), BF16),
        jax.ShapeDtypeStruct((n, D_FOURIER), BF16),
        jax.ShapeDtypeStruct((n, D_POOL), BF16),
        jax.ShapeDtypeStruct((n, 3 * D_MODEL), BF16),
    )
    specs = (
        pl.BlockSpec((N_SSM_GROUPS, tm // CHUNK, CHUNK * SSM_GROUP), lambda i: (0, i, 0)),
        pl.BlockSpec((tm, D_FOURIER), lambda i: (i, 0)),
        pl.BlockSpec((tm, D_POOL), lambda i: (i, 0)),
        pl.BlockSpec((tm, 3 * D_MODEL), lambda i: (i, 0)),
    )
    return shapes, specs


def _in_proj(h, g, w_in, tm):
    n = h.shape[0]
    shapes, specs = _in_proj_out(n, tm)
    return pl.pallas_call(
        _in_proj_body,
        grid=(n // tm,),
        in_specs=[
            pl.BlockSpec((tm, D_MODEL), lambda i: (i, 0)),
            _const_spec((1, D_MODEL)),
            _const_spec((D_MODEL, D_MIX + 3 * D_MODEL)),
        ],
        out_specs=specs,
        out_shape=shapes,
        scratch_shapes=[pltpu.VMEM((D_SSM // 128, tm, 128), F32)],
        compiler_params=_params(1),
        name="in_proj",
    )(h, g, w_in)


def _s5_prep_body(lre_ref, lim_ref, ls_ref, btr_ref, bti_ref, cr_ref, ci_ref, d_ref,
                  w_ref, qt_ref, km_ref, a_ref):
    is_fwd = (pl.program_id(1) == 0).astype(F32)
    lre = jnp.minimum(lre_ref[0, 0], -1e-4)
    lim = lim_ref[0, 0]
    step = jnp.exp(ls_ref[0, 0])
    mag = jnp.exp(lre * step)
    lbr = mag * jnp.cos(lim * step)
    lbi = mag * jnp.sin(lim * step)
    den = lre * lre + lim * lim
    nr = lbr - 1.0
    kr = (nr * lre + lbi * lim) / den
    ki = (lbi * lre - nr * lim) / den
    btr = btr_ref[0]
    bti = bti_ref[0]
    bbr = kr * btr - ki * bti
    bbi = kr * bti + ki * btr
    cr = cr_ref[0, 0]
    ci = ci_ref[0, 0]
    gb = cr.shape[0]
    eye = (
        lax.broadcasted_iota(jnp.int32, (gb, SSM_GROUP, SSM_GROUP), 1)
        == lax.broadcasted_iota(jnp.int32, (gb, SSM_GROUP, SSM_GROUP), 2)
    ).astype(F32)
    d_diag = eye * d_ref[0] * is_fwd
    pr = jnp.ones_like(lbr)
    pi = jnp.zeros_like(lbr)
    for j in range(CHUNK):
        rows = slice(j * SSM_GROUP, (j + 1) * SSM_GROUP)
        wr = pr * bbr - pi * bbi
        wi = pr * bbi + pi * bbr
        w_ref[0, 0, 0, :, rows, :] = wr
        w_ref[0, 0, 1, :, rows, :] = wi
        km = jnp.einsum("gap,gcp->gac", wr, cr, precision=lax.Precision.HIGHEST,
                        preferred_element_type=F32)
        km = km - jnp.einsum("gap,gcp->gac", wi, ci, precision=lax.Precision.HIGHEST,
                             preferred_element_type=F32)
        if j == 0:
            km = km + d_diag
        km_ref[0, 0, :, rows, :] = km
        pr, pi = pr * lbr - pi * lbi, pr * lbi + pi * lbr
        qt_ref[0, 0, 0, :, rows, :] = cr * pr - ci * pi
        qt_ref[0, 0, 1, :, rows, :] = -(cr * pi + ci * pr)
    a_ref[0, 0, 0] = pr
    a_ref[0, 0, 1] = pi


def _s5_prep(lam_re, lam_im, log_step, b_re, b_im, c_re, c_im, d_skip):
    g, p, gb = N_SSM_GROUPS, SSM_STATE, PREP_GROUPS
    lre = lam_re.reshape(DEPTH, 2, g, 1, p)
    lim = lam_im.reshape(DEPTH, 2, g, 1, p)
    ls = jnp.broadcast_to(log_step.reshape(DEPTH, 2, g, 1, 1), (DEPTH, 2, g, 1, p))
    btr = jnp.swapaxes(b_re, 2, 3)
    bti = jnp.swapaxes(b_im, 2, 3)
    d4 = d_skip.reshape(DEPTH, g, 1, SSM_GROUP)
    pole_spec = pl.BlockSpec((1, 1, gb, 1, p), lambda i, d, j: (i, d, j, 0, 0))
    b_spec = pl.BlockSpec((1, gb, SSM_GROUP, p), lambda i, d, j: (i, j, 0, 0))
    c_spec = pl.BlockSpec((1, 1, gb, SSM_GROUP, p), lambda i, d, j: (i, d, j, 0, 0))
    rows = CHUNK * SSM_GROUP
    return pl.pallas_call(
        _s5_prep_body,
        grid=(DEPTH, 2, g // gb),
        in_specs=[pole_spec, pole_spec, pole_spec, b_spec, b_spec, c_spec, c_spec,
                  pl.BlockSpec((1, gb, 1, SSM_GROUP), lambda i, d, j: (i, j, 0, 0))],
        out_specs=(
            pl.BlockSpec((1, 1, 2, gb, rows, p), lambda i, d, j: (i, d, 0, j, 0, 0)),
            pl.BlockSpec((1, 1, 2, gb, rows, p), lambda i, d, j: (i, d, 0, j, 0, 0)),
            pl.BlockSpec((1, 1, gb, rows, SSM_GROUP), lambda i, d, j: (i, d, j, 0, 0)),
            pl.BlockSpec((1, 1, 2, gb, 1, p), lambda i, d, j: (i, d, 0, j, 0, 0)),
        ),
        out_shape=(
            jax.ShapeDtypeStruct((DEPTH, 2, 2, g, rows, p), F32),
            jax.ShapeDtypeStruct((DEPTH, 2, 2, g, rows, p), F32),
            jax.ShapeDtypeStruct((DEPTH, 2, g, rows, SSM_GROUP), F32),
            jax.ShapeDtypeStruct((DEPTH, 2, 2, g, 1, p), F32),
        ),
        compiler_params=_params(3),
        name="s5_prep",
    )(lre, lim, ls, btr, bti, c_re, c_im, d4)


def _s5_assemble(w, qt, km, a16):
    g, p, t, c = N_SSM_GROUPS, SSM_STATE, CHUNK, SSM_GROUP
    w = w.reshape(DEPTH, 2, 2, g, t, c, p)
    qt = qt.reshape(DEPTH, 2, 2, g, t, c, p)
    pf = w[:, 0, :, :, ::-1]
    pb = w[:, 1]
    pcomp = jnp.stack([pf[:, 0], pf[:, 1], pb[:, 0], pb[:, 1]], axis=1)
    pcomp = pcomp.reshape(DEPTH, 4, g // 2, 2, t * c, p)
    pair_eye = jnp.eye(2, dtype=bool)[None, None, None, :, None, :, None]
    pmat = jnp.where(pair_eye, pcomp[:, :, :, :, :, None, :], 0.0)
    pmat = jnp.transpose(pmat, (0, 2, 3, 4, 1, 5, 6)).reshape(DEPTH, g // 2, 2 * t * c, 4 * 2 * p)
    qf = qt[:, 0]
    qb = qt[:, 1, :, :, ::-1]
    qcomp = jnp.stack([qf[:, 0], qf[:, 1], qb[:, 0], qb[:, 1]], axis=1)
    qcomp = qcomp.reshape(DEPTH, 4, g // 2, 2, t * c, p)
    qmat = jnp.where(pair_eye, qcomp[:, :, :, :, :, None, :], 0.0)
    qmat = jnp.transpose(qmat, (0, 2, 3, 1, 5, 6, 4)).reshape(DEPTH, g, 4 * 2 * p, t * c)
    km = km.reshape(DEPTH, 2, g, t, c, c)
    s_idx = np.arange(t)[:, None]
    t_idx = np.arange(t)[None, :]
    mf = jnp.where((t_idx >= s_idx)[None, None, :, :, None, None],
                   km[:, 0][:, :, np.clip(t_idx - s_idx, 0, t - 1)], 0.0)
    mb = jnp.where((t_idx <= s_idx)[None, None, :, :, None, None],
                   km[:, 1][:, :, np.clip(s_idx - t_idx, 0, t - 1)], 0.0)
    mmat = jnp.stack([mf, mb], axis=2)
    mmat = jnp.transpose(mmat, (0, 1, 2, 3, 5, 4, 6)).reshape(DEPTH, g, 2, t * c, t * c)
    acoef = a16.reshape(DEPTH, 4, g // 2, 2 * p)
    acoef = jnp.transpose(acoef, (0, 2, 1, 3))
    acoef = jnp.concatenate([acoef, jnp.zeros_like(acoef)], axis=2)
    return pmat.astype(BF16), mmat.astype(BF16), qmat.astype(BF16), acoef


def _cmul(ar, ai, br, bi):
    return ar * br - ai * bi, ar * bi + ai * br


def _cpow(ar, ai, n):
    rr, ri = jnp.ones_like(ar), jnp.zeros_like(ar)
    while n:
        if n & 1:
            rr, ri = _cmul(rr, ri, ar, ai)
        ar, ai = _cmul(ar, ai, ar, ai)
        n >>= 1
    return rr, ri


def _s5_body(x_ref, p_ref, m_ref, q_ref, a_ref, y_ref, sh_ref, *, n_chunks, n_seg, row_block):
    rows = n_chunks * SCAN_ROWS
    lanes = 2 * SSM_STATE
    pitch = n_chunks + SCAN_PAD
    seq_per_block = row_block // n_chunks

    def seq_rows(b):
        return slice(b * pitch, b * pitch + n_chunks)

    for r0 in range(0, rows, row_block):
        rs = slice(r0, r0 + row_block)
        s = _dot(x_ref[0, rs, :], p_ref[0, 0:256, :]) + _dot(x_ref[1, rs, :], p_ref[0, 256:512, :])
        for i in range(seq_per_block):
            for j in range(4):
                sh_ref[j, seq_rows(r0 // n_chunks + i), :] = s[i * n_chunks : (i + 1) * n_chunks, j * lanes : (j + 1) * lanes]

    def coef(i):
        return jnp.broadcast_to(a_ref[0, i : i + 1, :], (SCAN_ROWS, lanes))

    afr, afi, abr, abi = coef(0), coef(1), coef(2), coef(3)
    zero = jnp.zeros((SCAN_ROWS, lanes), F32)

    def tile(k, j):
        return (j, pl.ds(k, SCAN_ROWS, stride=pitch), slice(None))

    def scan(k, carry):
        fr, fi, br, bi = carry
        kb = n_chunks - 1 - k
        sfr, sfi = sh_ref[tile(k, 0)], sh_ref[tile(k, 1)]
        sbr, sbi = sh_ref[tile(kb, 2)], sh_ref[tile(kb, 3)]
        sh_ref[tile(k, 0)] = fr
        sh_ref[tile(k, 1)] = fi
        sh_ref[tile(kb, 2)] = br
        sh_ref[tile(kb, 3)] = bi
        nfr, nfi = _cmul(afr, afi, fr, fi)
        nbr, nbi = _cmul(abr, abi, br, bi)
        return nfr + sfr, nfi + sfi, nbr + sbr, nbi + sbi

    fr, fi, br, bi = lax.fori_loop(0, n_chunks, scan, (zero, zero, zero, zero))

    if n_seg > 1:
        seg = lax.broadcasted_iota(jnp.int32, (SCAN_ROWS, lanes), 0) % n_seg
        akfr, akfi = _cpow(afr, afi, n_chunks)
        akbr, akbi = _cpow(abr, abi, n_chunks)
        cfr, cfi, cbr, cbi = zero, zero, zero, zero
        for _ in range(n_seg - 1):
            tr, ti = _cmul(akfr, akfi, cfr, cfi)
            cfr = jnp.where(seg == 0, 0.0, pltpu.roll(tr + fr, 1, 0))
            cfi = jnp.where(seg == 0, 0.0, pltpu.roll(ti + fi, 1, 0))
            tr, ti = _cmul(akbr, akbi, cbr, cbi)
            cbr = jnp.where(seg == n_seg - 1, 0.0, pltpu.roll(tr + br, SCAN_ROWS - 1, 0))
            cbi = jnp.where(seg == n_seg - 1, 0.0, pltpu.roll(ti + bi, SCAN_ROWS - 1, 0))

        def fix(k, carry):
            pfr, pfi, pbr, pbi = carry
            kb = n_chunks - 1 - k
            dr, di = _cmul(pfr, pfi, cfr, cfi)
            sh_ref[tile(k, 0)] += dr
            sh_ref[tile(k, 1)] += di
            dr, di = _cmul(pbr, pbi, cbr, cbi)
            sh_ref[tile(kb, 2)] += dr
            sh_ref[tile(kb, 3)] += di
            return _cmul(afr, afi, pfr, pfi) + _cmul(abr, abi, pbr, pbi)

        one = jnp.ones((SCAN_ROWS, lanes), F32)
        lax.fori_loop(0, n_chunks, fix, (one, zero, one, zero))

    for r0 in range(0, rows, row_block):
        rs = slice(r0, r0 + row_block)
        hb = jnp.concatenate(
            [jnp.concatenate([sh_ref[j, seq_rows(r0 // n_chunks + i), :] for j in range(4)], axis=1)
             for i in range(seq_per_block)], axis=0).astype(BF16)
        for gi in range(2):
            x = x_ref[gi, rs, :]
            y_ref[gi, rs, :] = _dot(x, m_ref[gi, 0]) + _dot(x, m_ref[gi, 1]) + _dot(hb, q_ref[gi])


def _s5_mix(x, pmat, mmat, qmat, acoef, n_chunks, n_seg):
    g = N_SSM_GROUPS
    rows = n_chunks * SCAN_ROWS
    width = CHUNK * SSM_GROUP
    body = functools.partial(_s5_body, n_chunks=n_chunks, n_seg=n_seg, row_block=min(rows, 1024))
    return pl.pallas_call(
        body,
        grid=(g // 2,),
        in_specs=[
            pl.BlockSpec((2, rows, width), lambda i: (i, 0, 0)),
            pl.BlockSpec((1, 2 * width, 8 * SSM_STATE), lambda i: (i, 0, 0)),
            pl.BlockSpec((2, 2, width, width), lambda i: (i, 0, 0, 0)),
            pl.BlockSpec((2, 8 * SSM_STATE, width), lambda i: (i, 0, 0)),
            pl.BlockSpec((1, 8, 2 * SSM_STATE), lambda i: (i, 0, 0)),
        ],
        out_specs=pl.BlockSpec((2, rows, width), lambda i: (i, 0, 0)),
        out_shape=jax.ShapeDtypeStruct((g, rows, width), F32),
        scratch_shapes=[pltpu.VMEM((4, SCAN_ROWS * (n_chunks + SCAN_PAD), 2 * SSM_STATE), F32)],
        compiler_params=_params(1),
        name="s5_mix",
    )(x, pmat, mmat, qmat, acoef)


def _s5_layout(bsz, seq):
    n_seg = SCAN_ROWS // bsz
    n_chunks = seq // (CHUNK * n_seg)
    assert bsz * n_seg == SCAN_ROWS and n_chunks * n_seg * CHUNK == seq
    return n_seg, n_chunks


def _dft_split(seq):
    l1 = 1 << (int(math.log2(seq)) // 2)
    return l1, seq // l1


def _dft_tables(seq):
    l1, l2 = _dft_split(seq)
    k1 = (jnp.arange(l1)[:, None] * jnp.arange(l1)[None, :]) % l1
    th1 = k1.astype(F32) * (2.0 * math.pi / l1)
    f1cat = jnp.concatenate([jnp.cos(th1), -jnp.sin(th1)], axis=0)
    f = jnp.arange(l1)[:, None, None] + l1 * jnp.arange(l2)[None, :, None]
    k2 = (f * jnp.arange(l2)[None, None, :]) % seq
    th2 = k2.astype(F32) * (2.0 * math.pi / seq)
    c2, s2 = jnp.cos(th2), jnp.sin(th2)
    e2 = jnp.concatenate(
        [jnp.concatenate([c2, s2], axis=2), jnp.concatenate([-s2, c2], axis=2)], axis=1
    )
    return f1cat.astype(BF16), e2.astype(BF16)


def _channel_dft_table():
    n = FOURIER_GROUP
    th = ((jnp.arange(n)[:, None] * jnp.arange(n)[None, :]) % n).astype(F32) * (2.0 * math.pi / n)
    eye = jnp.eye(D_FOURIER // n, dtype=F32)
    return jnp.concatenate([jnp.kron(eye, jnp.cos(th)), jnp.kron(eye, jnp.sin(th))], axis=0).astype(BF16)


def _dft1_body(f_ref, v_ref, a_ref):
    a_ref[0] = _dot(f_ref[...], v_ref[0]).astype(BF16)


def _dft2_body(e_ref, a_ref, o_ref, *, n_f1):
    for i in range(n_f1):
        o_ref[0, :, i * D_FOURIER : (i + 1) * D_FOURIER] = _dot(e_ref[i], a_ref[0, i]).astype(BF16)


def _fourier_mix(u, f1cat, e2, bsz, seq):
    l1, l2 = _dft_split(seq)
    cols = l2 * D_FOURIER
    tn = min(cols, 4096)
    v = u.reshape(bsz, l1, cols)
    a = pl.pallas_call(
        _dft1_body,
        grid=(bsz, cols // tn),
        in_specs=[_const_spec((2 * l1, l1)), pl.BlockSpec((1, l1, tn), lambda b, j: (b, 0, j))],
        out_specs=pl.BlockSpec((1, 2 * l1, tn), lambda b, j: (b, 0, j)),
        out_shape=jax.ShapeDtypeStruct((bsz, 2 * l1, cols), BF16),
        compiler_params=_params(2),
        name="dft_stage1",
    )(f1cat, v)
    a = a.reshape(bsz, 2, l1, l2, D_FOURIER)
    a = jnp.transpose(a, (0, 2, 1, 3, 4)).reshape(bsz, l1, 2 * l2, D_FOURIER)
    n_f1 = 8
    w = pl.pallas_call(
        functools.partial(_dft2_body, n_f1=n_f1),
        grid=(l1 // n_f1, bsz),
        in_specs=[
            pl.BlockSpec((n_f1, 2 * l2, 2 * l2), lambda j, b: (j, 0, 0)),
            pl.BlockSpec((1, n_f1, 2 * l2, D_FOURIER), lambda j, b: (b, j, 0, 0)),
        ],
        out_specs=pl.BlockSpec((1, 2 * l2, n_f1 * D_FOURIER), lambda j, b: (b, 0, j)),
        out_shape=jax.ShapeDtypeStruct((bsz, 2 * l2, l1 * D_FOURIER), BF16),
        compiler_params=_params(2),
        name="dft_stage2",
    )(e2, a)
    return w.reshape(bsz, 2, seq, D_FOURIER)


def _pool_windows(up, prev, nxt, t0, seq):
    tm = up.shape[0]
    ext_rows = tm + 16
    ext = jnp.concatenate([prev, up, nxt], axis=0)
    s2 = ext + pltpu.roll(ext, 1, 0)
    s4 = s2 + pltpu.roll(s2, 2, 0)
    s8 = s4 + pltpu.roll(s4, 4, 0)
    s16 = s8 + pltpu.roll(s8, 8, 0)
    p2 = s2[8 : tm + 8]
    p4 = pltpu.roll(s4, ext_rows - 1, 0)[8 : tm + 8]
    p8 = pltpu.roll(s8, ext_rows - 3, 0)[8 : tm + 8]
    p16 = pltpu.roll(s16, ext_rows - 7, 0)[8 : tm + 8]
    grp = lax.broadcasted_iota(jnp.int32, (tm, D_POOL), 1) // POOL_GROUP
    t = t0 + lax.broadcasted_iota(jnp.int32, (tm, D_POOL), 0)
    half = jnp.left_shift(1, grp)
    cnt = jnp.minimum(t + half, seq) - jnp.maximum(t - half, 0)
    pooled = jnp.where(grp == 0, p2, jnp.where(grp == 1, p4, jnp.where(grp == 2, p8, p16)))
    return pooled / cnt.astype(F32) - up


def _merge_body(y_ref, w_ref, up_ref, upp_ref, upn_ref, gt_ref, h_ref,
                wglu_ref, bglu_ref, wups_ref, cs_ref, wupf_ref, wpool_ref, psc_ref, wupp_ref, wout_ref,
                o_ref, y_scr, *, tm, seq, four_scale):
    t0 = (pl.program_id(0) % (seq // tm)) * tm
    for o in range(D_SSM // 128):
        for half in range(2):
            grp = [y_ref[8 * o + j, :, 128 * half : 128 * (half + 1)] for j in range(8)]
            for s, t in enumerate(_block_transpose(grp)):
                y_scr[o, pl.ds(8 * half + s, tm // CHUNK, stride=CHUNK), :] = t
    g = jax.nn.gelu(jnp.concatenate([y_scr[o] for o in range(D_SSM // 128)], axis=1))
    s = g * jax.nn.sigmoid(_dot(g.astype(BF16), wglu_ref[...]) + bglu_ref[...])
    merged = gt_ref[:, 0:D_MODEL].astype(F32) * _dot(s.astype(BF16), wups_ref[...])
    four = _dot(w_ref[0, 0], cs_ref[0:D_FOURIER, :]) + _dot(w_ref[0, 1], cs_ref[D_FOURIER:, :])
    four = four * four_scale
    merged += gt_ref[:, D_MODEL : 2 * D_MODEL].astype(F32) * _dot(four.astype(BF16), wupf_ref[...])
    up = up_ref[...].astype(F32)
    prev = jnp.where(t0 == 0, 0.0, upp_ref[8:16, :].astype(F32))
    nxt = jnp.where(t0 + tm == seq, 0.0, upn_ref[0:8, :].astype(F32))
    pw = _pool_windows(up, prev, nxt, t0, seq)
    pm = _dot(pw.astype(BF16), wpool_ref[...]) * psc_ref[...]
    merged += gt_ref[:, 2 * D_MODEL :].astype(F32) * _dot(pm.astype(BF16), wupp_ref[...])
    o_ref[...] = h_ref[...] + _dot(merged.astype(BF16), wout_ref[...])


def _merge(y, w, up, gates, h, lw, bsz, seq, tm):
    n = bsz * seq
    per_seq = seq // tm
    halo = tm // 16
    body = functools.partial(_merge_body, tm=tm, seq=seq, four_scale=1.0 / math.sqrt(seq * FOURIER_GROUP))
    return pl.pallas_call(
        body,
        grid=(n // tm,),
        in_specs=[
            pl.BlockSpec((N_SSM_GROUPS, tm // CHUNK, CHUNK * SSM_GROUP), lambda i: (0, i, 0)),
            pl.BlockSpec((1, 2, tm, D_FOURIER), lambda i: (i // per_seq, 0, i % per_seq, 0)),
            pl.BlockSpec((tm, D_POOL), lambda i: (i, 0)),
            pl.BlockSpec((16, D_POOL), lambda i: (jnp.maximum(i * halo - 1, 0), 0)),
            pl.BlockSpec((16, D_POOL), lambda i: (jnp.minimum((i + 1) * halo, n // 16 - 1), 0)),
            pl.BlockSpec((tm, 3 * D_MODEL), lambda i: (i, 0)),
            pl.BlockSpec((tm, D_MODEL), lambda i: (i, 0)),
            _const_spec((D_SSM, D_SSM)),
            _const_spec((1, D_SSM)),
            _const_spec((D_SSM, D_MODEL)),
            _const_spec((2 * D_FOURIER, D_FOURIER)),
            _const_spec((D_FOURIER, D_MODEL)),
            _const_spec((D_POOL, D_POOL)),
            _const_spec((1, D_POOL)),
            _const_spec((D_POOL, D_MODEL)),
            _const_spec((D_MODEL, D_MODEL)),
        ],
        out_specs=pl.BlockSpec((tm, D_MODEL), lambda i: (i, 0)),
        out_shape=jax.ShapeDtypeStruct((n, D_MODEL), F32),
        scratch_shapes=[pltpu.VMEM((D_SSM // 128, tm, 128), F32)],
        compiler_params=_params(1),
        name="merge",
    )(y, w, up, up, up, gates, h, lw["w_glu"], lw["b_glu"], lw["w_up_ssm"], lw["cs"], lw["w_up_fourier"],
      lw["w_pool"], lw["pool_scale"], lw["w_up_pool"], lw["w_out"])


def _ffn_residual(h, nf_ref, wg_ref, wu_ref, wd_ref):
    hn = _rms(h, nf_ref[...]).astype(BF16)
    acc = jnp.zeros(h.shape, F32)
    for c0 in range(0, D_FF, FF_CHUNK):
        a = _dot(hn, wg_ref[:, c0 : c0 + FF_CHUNK])
        b = _dot(hn, wu_ref[:, c0 : c0 + FF_CHUNK])
        acc += _dot((jax.nn.silu(a) * b).astype(BF16), wd_ref[c0 : c0 + FF_CHUNK, :])
    return h + acc


def _ffn_next_body(h_ref, nf_ref, wg_ref, wu_ref, wd_ref, nm_ref, win_ref,
                   ho_ref, x_ref, uf_ref, up_ref, gt_ref, us_scr):
    h2 = _ffn_residual(h_ref[...], nf_ref, wg_ref, wu_ref, wd_ref)
    ho_ref[...] = h2
    _emit_in_proj(_rms(h2, nm_ref[...]).astype(BF16), win_ref, x_ref, uf_ref, up_ref, gt_ref, us_scr)


def _ffn_last_body(h_ref, nf_ref, wg_ref, wu_ref, wd_ref, nl_ref, o_ref):
    h2 = _ffn_residual(h_ref[...], nf_ref, wg_ref, wu_ref, wd_ref)
    o_ref[...] = _rms(h2, nl_ref[...])


def _ffn_specs(tm):
    return [
        pl.BlockSpec((tm, D_MODEL), lambda i: (i, 0)),
        _const_spec((1, D_MODEL)),
        _const_spec((D_MODEL, D_FF)),
        _const_spec((D_MODEL, D_FF)),
        _const_spec((D_FF, D_MODEL)),
        _const_spec((1, D_MODEL)),
    ]


def _ffn_next(h, lw, norm_next, w_in_next, tm):
    n = h.shape[0]
    shapes, specs = _in_proj_out(n, tm)
    return pl.pallas_call(
        _ffn_next_body,
        grid=(n // tm,),
        in_specs=_ffn_specs(tm) + [_const_spec((D_MODEL, D_MIX + 3 * D_MODEL))],
        out_specs=(pl.BlockSpec((tm, D_MODEL), lambda i: (i, 0)),) + specs,
        out_shape=(jax.ShapeDtypeStruct((n, D_MODEL), F32),) + shapes,
        scratch_shapes=[pltpu.VMEM((D_SSM // 128, tm, 128), F32)],
        compiler_params=_params(1),
        name="ffn_next",
    )(h, lw["norm_ffn"], lw["w_ff_gate"], lw["w_ff_up"], lw["w_ff_down"], norm_next, w_in_next)


def _ffn_last(h, lw, norm_final, tm):
    n = h.shape[0]
    return pl.pallas_call(
        _ffn_last_body,
        grid=(n // tm,),
        in_specs=_ffn_specs(tm),
        out_specs=pl.BlockSpec((tm, D_MODEL), lambda i: (i, 0)),
        out_shape=jax.ShapeDtypeStruct((n, D_MODEL), F32),
        compiler_params=_params(1),
        name="ffn_last",
    )(h, lw["norm_ffn"], lw["w_ff_gate"], lw["w_ff_up"], lw["w_ff_down"], norm_final)


def _trunk(x, layers, norm_mix, w_in, norm_final, cs):
    bsz, seq, _ = x.shape
    tm = TOKEN_TILE
    n_seg, n_chunks = _s5_layout(bsz, seq)
    f1cat, e2 = _dft_tables(seq)
    h = x.reshape(bsz * seq, D_MODEL)
    xs, uf, up, gates = _in_proj(h, norm_mix[0], w_in[0], tm)
    for i, lw in enumerate(layers):
        y = _s5_mix(xs, lw["s5_p"], lw["s5_m"], lw["s5_q"], lw["s5_a"], n_chunks, n_seg)
        w = _fourier_mix(uf, f1cat, e2, bsz, seq)
        h = _merge(y, w, up, gates, h, dict(lw, cs=cs), bsz, seq, tm)
        if i + 1 < DEPTH:
            h, xs, uf, up, gates = _ffn_next(h, lw, norm_mix[i + 1], w_in[i + 1], tm)
        else:
            h = _ffn_last(h, lw, norm_final, tm)
    return h.reshape(bsz, seq, D_MODEL)


def kernel(x_prompt, x_sample, norm_mix, w_in, ssm_lam_re, ssm_lam_im, ssm_log_step, ssm_b_re, ssm_b_im,
           ssm_c_re, ssm_c_im, ssm_d, w_glu, b_glu, w_up_ssm, w_up_fourier, w_pool, pool_scale, w_up_pool,
           w_out, norm_ffn, w_ff_gate, w_ff_up, w_ff_down, norm_final):
    s5_p, s5_m, s5_q, s5_a = _s5_assemble(
        *_s5_prep(ssm_lam_re, ssm_lam_im, ssm_log_step, ssm_b_re, ssm_b_im, ssm_c_re, ssm_c_im, ssm_d)
    )
    pool_eye = jnp.eye(D_POOL // POOL_GROUP, dtype=bool)[:, None, :, None]
    layers = []
    for i in range(DEPTH):
        w_pool_blk = jnp.where(pool_eye, w_pool[i][:, :, None, :], 0.0).reshape(D_POOL, D_POOL)
        layers.append(dict(
            s5_p=s5_p[i], s5_m=s5_m[i], s5_q=s5_q[i], s5_a=s5_a[i],
            w_glu=w_glu[i].astype(BF16), b_glu=b_glu[i].reshape(1, D_SSM),
            w_up_ssm=w_up_ssm[i].astype(BF16), w_up_fourier=w_up_fourier[i].astype(BF16),
            w_pool=w_pool_blk.astype(BF16), pool_scale=pool_scale[i].reshape(1, D_POOL),
            w_up_pool=w_up_pool[i].astype(BF16), w_out=w_out[i].astype(BF16),
            norm_ffn=norm_ffn[i].reshape(1, D_MODEL), w_ff_gate=w_ff_gate[i].astype(BF16),
            w_ff_up=w_ff_up[i].astype(BF16), w_ff_down=w_ff_down[i].astype(BF16),
        ))
    norm_mix = norm_mix.reshape(DEPTH, 1, D_MODEL)
    w_in = w_in.astype(BF16)
    norm_final = norm_final.reshape(1, D_MODEL)
    cs = _channel_dft_table()
    y_prompt = _trunk(x_prompt, layers, norm_mix, w_in, norm_final, cs)
    y_sample = _trunk(x_sample, layers, norm_mix, w_in, norm_final, cs)
    return (y_prompt, y_sample)
```
